```python
import math
import jax
import jax.numpy as jnp
from jax import lax
import numpy as np

D_MODEL = 4096
BATCH = 2
SEQ = 4096
DEPTH = 2

D_FF = 11008
FFN_RES = 0.5
N_MOD = 9
NORM_EPS = 1e-6
CONV_K = 4

DN_HEADS = 12
DN_DK = 128
DN_DV = 128
DN_CHUNK = 64

S5_CH = 1024
S5_GROUP = 16
S5_GROUPS = S5_CH // S5_GROUP
S5_STATE = 64

M2_HEADS = 24
M2_HEADDIM = 64
M2_INNER = M2_HEADS * M2_HEADDIM
M2_GROUPS = 4
M2_STATE = 128
M2_CHUNK = 128

DN_WIDTH = DN_HEADS * DN_DV
D_MIX = DN_WIDTH + S5_CH + M2_INNER
DN_QKV = 2 * DN_HEADS * DN_DK + DN_HEADS * DN_DV
M2_XBC = M2_INNER + 2 * M2_GROUPS * M2_STATE
IN_SIZES = (DN_QKV, DN_WIDTH, DN_HEADS, DN_HEADS, S5_CH, M2_INNER, M2_XBC, M2_HEADS)
D_IN = sum(IN_SIZES)

F32 = jnp.float32

kernel_name = "hymba_style_deltanet_s5_ssd_macaron_adaln"


def rms_norm(x, w):
    x32 = x.astype(F32)
    y = x32 * lax.rsqrt(jnp.mean(x32 * x32, axis=-1, keepdims=True) + NORM_EPS)
    return (y * w.astype(F32)).astype(x.dtype)


def l2_norm(x):
    return x * lax.rsqrt(jnp.sum(x * x, axis=-1, keepdims=True) + 1e-6)


def modulate(h, shift, scale):
    return h * (1 + scale) + shift


def swiglu(h, w_gate, w_up, w_down):
    return (jax.nn.silu(h @ w_gate) * (h @ w_up)) @ w_down


def causal_dwconv(x, w, b=None):
    k, ch = w.shape
    y = lax.conv_general_dilated(
        x, w[:, None, :].astype(x.dtype), window_strides=(1,), padding=[(k - 1, 0)],
        dimension_numbers=("NWC", "WIO", "NWC"), feature_group_count=ch)
    if b is not None:
        y = y + b.astype(x.dtype)
    return y


def segsum_exp(cs):
    n = cs.shape[-1]
    idx = jnp.arange(n)
    mask = idx[:, None] >= idx[None, :]
    return jnp.exp(jnp.where(mask, cs[..., :, None] - cs[..., None, :], -jnp.inf))


def gated_delta_rule(q, k, v, g, beta):
    bsz, l, h, dk = q.shape
    dv = v.shape[-1]
    c = DN_CHUNK
    nc = l // c
    to_chunks = lambda t: t.transpose(0, 2, 1, 3).reshape(bsz, h, nc, c, t.shape[-1])
    q, k, v = to_chunks(q), to_chunks(k), to_chunks(v)
    g = g.transpose(0, 2, 1).reshape(bsz, h, nc, c)
    beta = beta.transpose(0, 2, 1).reshape(bsz, h, nc, c)
    g = jnp.cumsum(g, axis=-1)
    decay = segsum_exp(g)
    k_beta = k * beta[..., None]
    v_beta = v * beta[..., None]
    idx = jnp.arange(c)
    strict = idx[:, None] > idx[None, :]
    m = jnp.where(strict, jnp.einsum("bhnid,bhnjd->bhnij", k_beta, k) * decay, 0.0)
    a_mat = jnp.eye(c, dtype=q.dtype) + m
    rhs = jnp.concatenate([v_beta, k_beta * jnp.exp(g)[..., None]], axis=-1)
    sol = lax.linalg.triangular_solve(a_mat, rhs, left_side=True, lower=True)
    u, w = sol[..., :dv], sol[..., dv:]
    qk = jnp.einsum("bhnid,bhnjd->bhnij", q, k) * decay
    g_last = g[..., -1]
    q_dec = q * jnp.exp(g)[..., None]
    k_dec = k * jnp.exp(g_last[..., None] - g)[..., None]

    def step(s, inp):
        q_n, qk_n, u_n, w_n, k_n, gl_n = inp
        v_new = u_n - jnp.einsum("bhck,bhkv->bhcv", w_n, s)
        o = jnp.einsum("bhck,bhkv->bhcv", q_n, s) + jnp.einsum("bhij,bhjv->bhiv", qk_n, v_new)
        s = s * jnp.exp(gl_n)[..., None, None] + jnp.einsum("bhck,bhcv->bhkv", k_n, v_new)
        return s, o

    xs = tuple(jnp.moveaxis(t, 2, 0) for t in (q_dec, qk, u, w, k_dec, g_last))
    s0 = jnp.zeros((bsz, h, dk, dv), q.dtype)
    _, o = lax.scan(step, s0, xs)
    return jnp.moveaxis(o, 0, 2).reshape(bsz, h, l, dv).transpose(0, 2, 1, 3)


def gated_deltanet(qkv, z, alpha, beta_logit, conv_w, a_log, dt_bias, norm_w):
    bsz, l, _ = qkv.shape
    qkv = jax.nn.silu(causal_dwconv(qkv, conv_w)).astype(F32)
    q, k, v = jnp.split(qkv, [DN_HEADS * DN_DK, 2 * DN_HEADS * DN_DK], axis=-1)
    q = l2_norm(q.reshape(bsz, l, DN_HEADS, DN_DK)) * (DN_DK ** -0.5)
    k = l2_norm(k.reshape(bsz, l, DN_HEADS, DN_DK))
    v = v.reshape(bsz, l, DN_HEADS, DN_DV)
    beta = jax.nn.sigmoid(beta_logit.astype(F32))
    g = -jnp.exp(a_log.astype(F32)) * jax.nn.softplus(alpha.astype(F32) + dt_bias.astype(F32))
    o = gated_delta_rule(q, k, v, g, beta)
    o = rms_norm(o, norm_w) * jax.nn.silu(z.astype(F32).reshape(bsz, l, DN_HEADS, DN_DV))
    return o.reshape(bsz, l, DN_WIDTH)


def complex_affine_combine(e1, e2):
    a1r, a1i, b1r, b1i = e1
    a2r, a2i, b2r, b2i = e2
    return (a1r * a2r - a1i * a2i, a1r * a2i + a1i * a2r,
            a2r * b1r - a2i * b1i + b2r, a2r * b1i + a2i * b1r + b2i)


def s5_mixer(u, a_re, a_im, b_re, b_im, c_re, c_im, log_dt, d, w_glu, b_glu):
    bsz, l, _ = u.shape
    u32 = u.astype(F32)
    ug = u32.reshape(bsz, l, S5_GROUPS, S5_GROUP)
    lr, li = a_re.astype(F32), a_im.astype(F32)
    dt = jnp.exp(log_dt.astype(F32))[:, None]
    mag = jnp.exp(lr * dt)
    ab_re, ab_im = mag * jnp.cos(li * dt), mag * jnp.sin(li * dt)
    den = lr * lr + li * li
    er, ei = ab_re - 1.0, ab_im
    fr = ((lr * er + li * ei) / den)[..., None]
    fi = ((lr * ei - li * er) / den)[..., None]
    br, bi = b_re.astype(F32), b_im.astype(F32)
    bb_re = fr * br - fi * bi
    bb_im = fr * bi + fi * br
    bu_re = jnp.einsum("gpc,blgc->blgp", bb_re, ug)
    bu_im = jnp.einsum("gpc,blgc->blgp", bb_im, ug)
    elems = (jnp.broadcast_to(ab_re, bu_re.shape), jnp.broadcast_to(ab_im, bu_im.shape), bu_re, bu_im)
    _, _, s_re, s_im = lax.associative_scan(complex_affine_combine, elems, axis=1)
    y = (jnp.einsum("gcp,blgp->blgc", c_re.astype(F32), s_re)
         - jnp.einsum("gcp,blgp->blgc", c_im.astype(F32), s_im))
    y = y.reshape(bsz, l, S5_CH) + d.astype(F32) * u32
    hg = jax.nn.gelu(y) @ w_glu + b_glu
    val, gate = jnp.split(hg, 2, axis=-1)
    return val * jax.nn.sigmoid(gate)


def ssd_chunked(x, a, bm, cm):
    bsz, l, g, j, p = x.shape
    c = M2_CHUNK
    nc = l // c
    x = x.reshape(bsz, nc, c, g, j, p)
    bm = bm.reshape(bsz, nc, c, g, -1)
    cm = cm.reshape(bsz, nc, c, g, -1)
    a = a.reshape(bsz, nc, c, g, j).transpose(0, 3, 4, 1, 2)
    a_cs = jnp.cumsum(a, axis=-1)
    seg = segsum_exp(a_cs)
    cb = jnp.einsum("bzlgn,bzsgn->bgzls", cm, bm)
    y_diag = jnp.einsum("bgjzls,bzsgjp->bzlgjp", cb[:, :, None] * seg, x)
    decay_states = jnp.exp(a_cs[..., -1:] - a_cs)
    states = jnp.einsum("bzsgn,bgjzs,bzsgjp->bzgjpn", bm, decay_states, x)
    a_last = jnp.pad(a_cs[..., -1], ((0, 0), (0, 0), (0, 0), (1, 0)))
    decay_chunk = segsum_exp(jnp.cumsum(a_last, axis=-1))
    states = jnp.concatenate([jnp.zeros_like(states[:, :1]), states], axis=1)
    start_states = jnp.einsum("bgjyz,bzgjpn->bygjpn", decay_chunk, states)[:, :-1]
    y_off = jnp.einsum("bzlgn,bzgjpn,bgjzl->bzlgjp", cm, start_states, jnp.exp(a_cs))
    return (y_diag + y_off).reshape(bsz, l, g, j, p)


def mamba2_mixer(z, xbc, dt_raw, conv_w, conv_b, a_log, dt_bias, d, norm_w):
    bsz, l, _ = z.shape
    hpg = M2_HEADS // M2_GROUPS
    xbc = jax.nn.silu(causal_dwconv(xbc, conv_w, conv_b)).astype(F32)
    xs, bm, cm = jnp.split(xbc, [M2_INNER, M2_INNER + M2_GROUPS * M2_STATE], axis=-1)
    xs = xs.reshape(bsz, l, M2_GROUPS, hpg, M2_HEADDIM)
    bm = bm.reshape(bsz, l, M2_GROUPS, M2_STATE)
    cm = cm.reshape(bsz, l, M2_GROUPS, M2_STATE)
    dt = jax.nn.softplus(dt_raw.astype(F32) + dt_bias.astype(F32)).reshape(bsz, l, M2_GROUPS, hpg)
    a = -jnp.exp(a_log.astype(F32)).reshape(M2_GROUPS, hpg)
    y = ssd_chunked(xs * dt[..., None], a * dt, bm, cm)
    y = y + d.astype(F32).reshape(M2_GROUPS, hpg, 1) * xs
    y = y.reshape(bsz, l, M2_INNER) * jax.nn.silu(z.astype(F32))
    y = rms_norm(y.reshape(bsz, l, M2_GROUPS, M2_INNER // M2_GROUPS), norm_w.reshape(M2_GROUPS, -1))
    return y.reshape(bsz, l, M2_INNER)


def _dt_bias_init(key, shape):
    dt = jnp.exp(jax.random.uniform(key, shape, minval=math.log(1e-3), maxval=math.log(1e-1)))
    return dt + jnp.log(-jnp.expm1(-dt))


def setup_inputs(seed: int = 0) -> dict:
    key = jax.random.key(seed)
    ks = iter(jax.random.split(key, 40))
    nrm = lambda shape, s: s * jax.random.normal(next(ks), shape, F32)
    gain = lambda shape: 1.0 + 0.02 * jax.random.normal(next(ks), shape, F32)
    L = DEPTH
    s5_n = jnp.arange(S5_STATE, dtype=F32)
    return {
        "x": nrm((BATCH, SEQ, D_MODEL), 1.0),
        "c": nrm((BATCH, D_MODEL), 1.0),
        "w_mod": nrm((D_MODEL, N_MOD * D_MODEL), 0.5 * D_MODEL ** -0.5),
        "b_mod": nrm((N_MOD * D_MODEL,), 0.02),
        "mod_table": nrm((L, N_MOD, D_MODEL), 0.1),
        "norm_ffn1": gain((L, D_MODEL)),
        "norm_mix": gain((L, D_MODEL)),
        "norm_ffn2": gain((L, D_MODEL)),
        "ffn1_w_gate": nrm((L, D_MODEL, D_FF), D_MODEL ** -0.5),
        "ffn1_w_up": nrm((L, D_MODEL, D_FF), D_MODEL ** -0.5),
        "ffn1_w_down": nrm((L, D_FF, D_MODEL), D_FF ** -0.5),
        "ffn2_w_gate": nrm((L, D_MODEL, D_FF), D_MODEL ** -0.5),
        "ffn2_w_up": nrm((L, D_MODEL, D_FF), D_MODEL ** -0.5),
        "ffn2_w_down": nrm((L, D_FF, D_MODEL), D_FF ** -0.5),
        "w_in": nrm((L, D_MODEL, D_IN), D_MODEL ** -0.5),
        "w_out": nrm((L, D_MIX, D_MODEL), D_MIX ** -0.5),
        "dn_conv_w": nrm((L, CONV_K, DN_QKV), CONV_K ** -0.5),
        "dn_a_log": jnp.log(jax.random.uniform(next(ks), (L, DN_HEADS), F32, 1.0, 16.0)),
        "dn_dt_bias": _dt_bias_init(next(ks), (L, DN_HEADS)),
        "dn_norm_w": gain((L, DN_DV)),
        "s5_a_re": -0.5 * jnp.exp(nrm((L, S5_GROUPS, S5_STATE), 0.01)),
        "s5_a_im": jnp.pi * s5_n + nrm((L, S5_GROUPS, S5_STATE), 0.01),
        "s5_b_re": nrm((L, S5_GROUPS, S5_STATE, S5_GROUP), (2 * S5_GROUP) ** -0.5),
        "s5_b_im": nrm((L, S5_GROUPS, S5_STATE, S5_GROUP), (2 * S5_GROUP) ** -0.5),
        "s5_c_re": nrm((L, S5_GROUPS, S5_GROUP, S5_STATE), 0.5),
        "s5_c_im": nrm((L, S5_GROUPS, S5_GROUP, S5_STATE), 0.5),
        "s5_log_dt": jax.random.uniform(next(ks), (L, S5_GROUPS), F32, math.log(1e-3), math.log(1e-1)),
        "s5_d": nrm((L, S5_CH), 1.0),
        "s5_w_glu": nrm((L, S5_CH, 2 * S5_CH), S5_CH ** -0.5),
        "s5_b_glu": nrm((L, 2 * S5_CH), 0.02),
        "m2_conv_w": nrm((L, CONV_K, M2_XBC), CONV_K ** -0.5),
        "m2_conv_b": nrm((L, M2_XBC), 0.02),
        "m2_a_log": jnp.log(jax.random.uniform(next(ks), (L, M2_HEADS), F32, 1.0, 16.0)),
        "m2_dt_bias": _dt_bias_init(next(ks), (L, M2_HEADS)),
        "m2_d": gain((L, M2_HEADS)),
        "m2_norm_w": gain((L, M2_INNER)),
        "final_norm": gain((D_MODEL,)),
    }


def reference(x, c, w_mod, b_mod, mod_table, norm_ffn1, norm_mix, norm_ffn2,
              ffn1_w_gate, ffn1_w_up, ffn1_w_down, ffn2_w_gate, ffn2_w_up, ffn2_w_down,
              w_in, w_out, dn_conv_w, dn_a_log, dn_dt_bias, dn_norm_w,
              s5_a_re, s5_a_im, s5_b_re, s5_b_im, s5_c_re, s5_c_im, s5_log_dt, s5_d,
              s5_w_glu, s5_b_glu, m2_conv_w, m2_conv_b, m2_a_log, m2_dt_bias, m2_d,
              m2_norm_w, final_norm):
    bsz = x.shape[0]
    mod_shared = (jax.nn.silu(c.astype(F32)) @ w_mod + b_mod).reshape(bsz, N_MOD, D_MODEL)
    split_points = np.cumsum(IN_SIZES)[:-1].tolist()
    for i in range(DEPTH):
        mod = (mod_shared + mod_table[i]).astype(x.dtype)
        sh1, sc1, g1, sh2, sc2, g2, sh3, sc3, g3 = [mod[:, k, None, :] for k in range(N_MOD)]
        h = modulate(rms_norm(x, norm_ffn1[i]), sh1, sc1)
        x = x + FFN_RES * g1 * swiglu(h, ffn1_w_gate[i], ffn1_w_up[i], ffn1_w_down[i])
        h = modulate(rms_norm(x, norm_mix[i]), sh2, sc2)
        proj = h @ w_in[i]
        dn_qkv, dn_z, dn_alpha, dn_beta, s5_u, m2_z, m2_xbc, m2_dt = jnp.split(proj, split_points, axis=-1)
        y_dn = gated_deltanet(dn_qkv, dn_z, dn_alpha, dn_beta, dn_conv_w[i], dn_a_log[i],
                              dn_dt_bias[i], dn_norm_w[i])
        y_s5 = s5_mixer(s5_u, s5_a_re[i], s5_a_im[i], s5_b_re[i], s5_b_im[i], s5_c_re[i],
                        s5_c_im[i], s5_log_dt[i], s5_d[i], s5_w_glu[i], s5_b_glu[i])
        y_m2 = mamba2_mixer(m2_z, m2_xbc, m2_dt, m2_conv_w[i], m2_conv_b[i], m2_a_log[i],
                            m2_dt_bias[i], m2_d[i], m2_norm_w[i])
        y_mix = jnp.concatenate([y_dn, y_s5, y_m2], axis=-1).astype(h.dtype)
        x = x + g2 * (y_mix @ w_out[i])
        h = modulate(rms_norm(x, norm_ffn2[i]), sh3, sc3)
        x = x + FFN_RES * g3 * swiglu(h, ffn2_w_gate[i], ffn2_w_up[i], ffn2_w_down[i])
    return rms_norm(x, final_norm)
```

```python
import functools

import jax
import jax.numpy as jnp
import numpy as np
from jax import lax
from jax.experimental import pallas as pl
from jax.experimental.pallas import tpu as pltpu

F32 = jnp.float32
BF16 = jnp.bfloat16

D_MODEL = 4096
DEPTH = 2
D_FF = 11008
FFN_RES = 0.5
N_MOD = 9
NORM_EPS = 1e-6
CONV_K = 4

DN_HEADS = 12
DN_DK = 128
DN_DV = 128
DN_CHUNK = 64
DN_BLOCK = 256

S5_CH = 1024
S5_GROUP = 16
S5_GROUPS = 64
S5_STATE = 64
S5_LANES = S5_GROUPS * S5_STATE
S5_TILE_G = 8
S5_ROWS = 8
S5_BLOCK = 256

M2_HEADS = 24
M2_HEADDIM = 64
M2_INNER = M2_HEADS * M2_HEADDIM
M2_GROUPS = 4
M2_HPG = M2_HEADS // M2_GROUPS
M2_STATE = 128
M2_CHUNK = 128

DN_WIDTH = DN_HEADS * DN_DV
DN_QKV = 2 * DN_HEADS * DN_DK + DN_HEADS * DN_DV
M2_XBC = M2_INNER + 2 * M2_GROUPS * M2_STATE
IN_SIZES = (DN_QKV, DN_WIDTH, DN_HEADS, DN_HEADS, S5_CH, M2_INNER, M2_XBC, M2_HEADS)
SMALL_W = 128
SMALL_ROWS = 48

LANE = 128
VMEM_LIMIT = 56 * 1024 * 1024


def _cparams(n_axes):
    return pltpu.CompilerParams(
        dimension_semantics=("arbitrary",) * n_axes, vmem_limit_bytes=VMEM_LIMIT)


def _sigmoid(x):
    return 1.0 / (1.0 + jnp.exp(-x))


def _silu(x):
    return x * _sigmoid(x)


def _softplus(x):
    return jnp.maximum(x, 0.0) + jnp.log(1.0 + jnp.exp(-jnp.abs(x)))


def _dot(a, b):
    return jnp.dot(a.astype(BF16), b.astype(BF16), preferred_element_type=F32)


def _dot_nt(a, b):
    return lax.dot_general(a.astype(BF16), b.astype(BF16), (((1,), (1,)), ((), ())),
                           preferred_element_type=F32)


def _dot_tn(a, b):
    return lax.dot_general(a.astype(BF16), b.astype(BF16), (((0,), (0,)), ((), ())),
                           preferred_element_type=F32)


def _split3(x):
    x1 = x.astype(BF16)
    r = x - x1.astype(F32)
    x2 = r.astype(BF16)
    x3 = (r - x2.astype(F32)).astype(BF16)
    return x1, x2, x3


def _dot_exact_lhs(mask_bf16, x):
    x1, x2, x3 = _split3(x)
    d = lambda t: jnp.dot(mask_bf16, t, preferred_element_type=F32)
    return d(x1) + d(x2) + d(x3)


def _dot_exact_rhs(x, mask_bf16):
    x1, x2, x3 = _split3(x)
    d = lambda t: jnp.dot(t, mask_bf16, preferred_element_type=F32)
    return d(x1) + d(x2) + d(x3)


def _dot_hi(a, b):
    a1 = a.astype(BF16)
    a2 = (a - a1.astype(F32)).astype(BF16)
    b1 = b.astype(BF16)
    b2 = (b - b1.astype(F32)).astype(BF16)
    d = lambda s, t: jnp.dot(s, t, preferred_element_type=F32)
    return d(a1, b1) + (d(a1, b2) + d(a2, b1))


def _mod_kernel(c_ref, w_ref, b_ref, t_ref, o_ref):
    acc = _dot(_silu(c_ref[...]), w_ref[...]) + b_ref[...]
    for l in range(DEPTH):
        o_ref[l] = acc + t_ref[l]


def _modulation(c, w_mod, b_mod, mod_table):
    bsz = c.shape[0]
    rows = 8
    c_pad = jnp.zeros((rows, D_MODEL), F32).at[:bsz].set(c.astype(F32))
    n = N_MOD * D_MODEL
    tn = 512
    out = pl.pallas_call(
        _mod_kernel,
        grid=(n // tn,),
        in_specs=[
            pl.BlockSpec((rows, D_MODEL), lambda j: (0, 0)),
            pl.BlockSpec((D_MODEL, tn), lambda j: (0, j)),
            pl.BlockSpec((1, tn), lambda j: (0, j)),
            pl.BlockSpec((DEPTH, 1, tn), lambda j: (0, 0, j)),
        ],
        out_specs=pl.BlockSpec((DEPTH, rows, tn), lambda j: (0, 0, j)),
        out_shape=jax.ShapeDtypeStruct((DEPTH, rows, n), F32),
        compiler_params=_cparams(1),
        name="adaln_mod",
    )(c_pad, w_mod, b_mod.reshape(1, n), mod_table.reshape(DEPTH, 1, n))
    return out[:, :bsz].reshape(DEPTH, bsz * N_MOD, 1, D_MODEL)


def _normmod_kernel(x_ref, w_ref, sh_ref, sc_ref, o_ref):
    x = x_ref[...]
    y = x * lax.rsqrt(jnp.mean(x * x, axis=-1, keepdims=True) + NORM_EPS) * w_ref[...]
    o_ref[...] = (y * (1.0 + sc_ref[...]) + sh_ref[...]).astype(o_ref.dtype)


def _norm_kernel(x_ref, w_ref, o_ref):
    x = x_ref[...]
    y = x * lax.rsqrt(jnp.mean(x * x, axis=-1, keepdims=True) + NORM_EPS) * w_ref[...]
    o_ref[...] = y.astype(o_ref.dtype)


def _norm_modulate(x, w, modl, k_shift, k_scale, seq):
    m = x.shape[0]
    tm = 256
    per_b = seq // tm
    vec = lambda k: pl.BlockSpec((None, 1, D_MODEL), lambda i: ((i // per_b) * N_MOD + k, 0, 0))
    return pl.pallas_call(
        _normmod_kernel,
        grid=(m // tm,),
        in_specs=[
            pl.BlockSpec((tm, D_MODEL), lambda i: (i, 0)),
            pl.BlockSpec((1, D_MODEL), lambda i: (0, 0)),
            vec(k_shift),
            vec(k_scale),
        ],
        out_specs=pl.BlockSpec((tm, D_MODEL), lambda i: (i, 0)),
        out_shape=jax.ShapeDtypeStruct((m, D_MODEL), BF16),
        compiler_params=_cparams(1),
        name="norm_modulate",
    )(x, w.reshape(1, D_MODEL), modl, modl)


def _final_norm(x, w):
    m = x.shape[0]
    tm = 256
    return pl.pallas_call(
        _norm_kernel,
        grid=(m // tm,),
        in_specs=[
            pl.BlockSpec((tm, D_MODEL), lambda i: (i, 0)),
            pl.BlockSpec((1, D_MODEL), lambda i: (0, 0)),
        ],
        out_specs=pl.BlockSpec((tm, D_MODEL), lambda i: (i, 0)),
        out_shape=jax.ShapeDtypeStruct((m, D_MODEL), F32),
        compiler_params=_cparams(1),
        name="final_norm",
    )(x, w.reshape(1, D_MODEL))


def _mm_kernel(a_ref, w_ref, o_ref):
    o_ref[...] = jnp.dot(a_ref[...], w_ref[...], preferred_element_type=F32).astype(o_ref.dtype)


def _matmul(a, w, tn, out_dtype=F32, name="matmul"):
    m, k = a.shape
    n = w.shape[1]
    tm = min(m, 1024)
    return pl.pallas_call(
        _mm_kernel,
        grid=(m // tm, n // tn),
        in_specs=[
            pl.BlockSpec((tm, k), lambda i, j: (i, 0)),
            pl.BlockSpec((k, tn), lambda i, j: (0, j)),
        ],
        out_specs=pl.BlockSpec((tm, tn), lambda i, j: (i, j)),
        out_shape=jax.ShapeDtypeStruct((m, n), out_dtype),
        compiler_params=_cparams(2),
        name=name,
    )(a, w)


def _gateup_kernel(h_ref, wg_ref, wu_ref, o_ref):
    h = h_ref[...]
    g = jnp.dot(h, wg_ref[...], preferred_element_type=F32)
    u = jnp.dot(h, wu_ref[...], preferred_element_type=F32)
    o_ref[...] = (_silu(g) * u).astype(o_ref.dtype)


def _ffn_gateup(h, wg, wu):
    m = h.shape[0]
    tm = min(m, 1024)
    tn = 256
    return pl.pallas_call(
        _gateup_kernel,
        grid=(m // tm, D_FF // tn),
        in_specs=[
            pl.BlockSpec((tm, D_MODEL), lambda i, j: (i, 0)),
            pl.BlockSpec((D_MODEL, tn), lambda i, j: (0, j)),
            pl.BlockSpec((D_MODEL, tn), lambda i, j: (0, j)),
        ],
        out_specs=pl.BlockSpec((tm, tn), lambda i, j: (i, j)),
        out_shape=jax.ShapeDtypeStruct((m, D_FF), BF16),
        compiler_params=_cparams(2),
        name="ffn_gateup",
    )(h, wg, wu)


def _down_kernel(a_ref, w_ref, x_ref, g_ref, o_ref, acc_ref):
    k = pl.program_id(2)

    @pl.when(k == 0)
    def _():
        acc_ref[...] = jnp.zeros_like(acc_ref)

    acc_ref[...] += jnp.dot(a_ref[...], w_ref[...], preferred_element_type=F32)

    @pl.when(k == pl.num_programs(2) - 1)
    def _():
        o_ref[...] = x_ref[...] + FFN_RES * g_ref[...] * acc_ref[...]


def _ffn_down(a, wd, x, modl, k_gate, seq):
    m = a.shape[0]
    tm = min(m, 1024)
    tn = 512
    tk = D_FF // 2
    per_b = seq // tm
    return pl.pallas_call(
        _down_kernel,
        grid=(m // tm, D_MODEL // tn, D_FF // tk),
        in_specs=[
            pl.BlockSpec((tm, tk), lambda i, j, k: (i, k)),
            pl.BlockSpec((tk, tn), lambda i, j, k: (k, j)),
            pl.BlockSpec((tm, tn), lambda i, j, k: (i, j)),
            pl.BlockSpec((None, 1, tn), lambda i, j, k: ((i // per_b) * N_MOD + k_gate, 0, j)),
        ],
        out_specs=pl.BlockSpec((tm, tn), lambda i, j, k: (i, j)),
        out_shape=jax.ShapeDtypeStruct((m, D_MODEL), F32),
        scratch_shapes=[pltpu.VMEM((tm, tn), F32)],
        compiler_params=_cparams(3),
        name="ffn_down",
    )(a, wd, x, modl)


def _outproj_kernel(ydn_ref, ys5_ref, ym2_ref, w_ref, x_ref, g_ref, o_ref):
    acc = jnp.dot(ydn_ref[...], w_ref[0:DN_WIDTH, :], preferred_element_type=F32)
    acc += jnp.dot(ys5_ref[...], w_ref[DN_WIDTH:DN_WIDTH + S5_CH, :], preferred_element_type=F32)
    acc += jnp.dot(ym2_ref[...], w_ref[DN_WIDTH + S5_CH:, :], preferred_element_type=F32)
    o_ref[...] = x_ref[...] + g_ref[...] * acc


def _out_proj(y_dn, y_s5, y_m2, w, x, modl, k_gate, seq):
    m = x.shape[0]
    tm = min(m, 1024)
    tn = 512
    per_b = seq // tm
    return pl.pallas_call(
        _outproj_kernel,
        grid=(m // tm, D_MODEL // tn),
        in_specs=[
            pl.BlockSpec((tm, DN_WIDTH), lambda i, j: (i, 0)),
            pl.BlockSpec((tm, S5_CH), lambda i, j: (i, 0)),
            pl.BlockSpec((tm, M2_INNER), lambda i, j: (i, 0)),
            pl.BlockSpec((D_MODEL, tn), lambda i, j: (0, j)),
            pl.BlockSpec((tm, tn), lambda i, j: (i, j)),
            pl.BlockSpec((None, 1, tn), lambda i, j: ((i // per_b) * N_MOD + k_gate, 0, j)),
        ],
        out_specs=pl.BlockSpec((tm, tn), lambda i, j: (i, j)),
        out_shape=jax.ShapeDtypeStruct((m, D_MODEL), F32),
        compiler_params=_cparams(2),
        name="out_proj",
    )(y_dn, y_s5, y_m2, w, x, modl)


def _causal_conv(buf_ref, raw, w, t):
    buf_ref[8:t + 8, :] = raw
    acc = raw * w[CONV_K - 1:CONV_K, :]
    for j in range(CONV_K - 1):
        off = 8 - (CONV_K - 1) + j
        acc += buf_ref[off:off + t, :] * w[j:j + 1, :]
    buf_ref[0:8, :] = raw[t - 8:t, :]
    return acc


def _dn_kernel(q_ref, k_ref, v_ref, z_ref, sm_ref, smt_ref, wq_ref, wk_ref, wv_ref,
               alog_ref, dtb_ref, nw_ref, o_ref, s_ref, buf_ref, vn_ref):
    tb, ck = DN_BLOCK, DN_CHUNK
    h = pl.program_id(1)

    @pl.when(pl.program_id(2) == 0)
    def _():
        s_ref[...] = jnp.zeros_like(s_ref)
        for i in range(3):
            buf_ref[i, 0:8, :] = jnp.zeros((8, LANE), F32)

    q = _silu(_causal_conv(buf_ref.at[0], q_ref[...], wq_ref[...], tb))
    k = _silu(_causal_conv(buf_ref.at[1], k_ref[...], wk_ref[...], tb))
    v = _silu(_causal_conv(buf_ref.at[2], v_ref[...], wv_ref[...], tb))
    q = q * lax.rsqrt(jnp.sum(q * q, axis=-1, keepdims=True) + 1e-6) * (DN_DK ** -0.5)
    k = k * lax.rsqrt(jnp.sum(k * k, axis=-1, keepdims=True) + 1e-6)

    lane = lax.broadcasted_iota(jnp.int32, (1, SMALL_W), 1)
    sm = sm_ref[...]
    pick = lambda idx: jnp.sum(jnp.where(lane == idx, sm, 0.0), axis=1, keepdims=True)
    alpha_c, betal_c = pick(h), pick(DN_HEADS + h)
    alpha_r = smt_ref[pl.ds(h, 1), :]
    betal_r = smt_ref[pl.ds(DN_HEADS + h, 1), :]
    one = jnp.ones((1, 1), F32)
    coef = -jnp.exp(one * alog_ref[h])
    dtb = one * dtb_ref[h]
    g_c = coef * _softplus(alpha_c + dtb)
    g_r = coef * _softplus(alpha_r + dtb)
    beta_c = _sigmoid(betal_c)
    del betal_r

    row = lax.broadcasted_iota(jnp.int32, (tb, tb), 0)
    col = lax.broadcasted_iota(jnp.int32, (tb, tb), 1)
    same = (row // ck) == (col // ck)
    lower = same & (row >= col)
    strict = same & (row > col)
    tri = jnp.where(lower, 1.0, 0.0).astype(BF16)
    tri_t = jnp.where(same & (row <= col), 1.0, 0.0).astype(BF16)

    gc_c = _dot_exact_lhs(tri, jnp.broadcast_to(g_c, (tb, tb)))
    gc_r = _dot_exact_rhs(jnp.broadcast_to(g_r, (8, tb)), tri_t)
    gc_rr = jnp.broadcast_to(gc_r[0:1, :], (tb, tb))
    decay = jnp.exp(jnp.where(lower, gc_c - gc_rr, -1e30))
    gcol = gc_c[:, 0:LANE]
    egc = jnp.exp(gcol)

    kb = k * beta_c
    vb = v * beta_c
    mm = jnp.where(strict, _dot_nt(kb, k) * decay, 0.0)

    blk = lambda s: (row // s) == (col // s)
    d1 = jnp.where(blk(8), mm, 0.0)
    d2 = _dot_hi(d1, d1)
    d4 = _dot_hi(d2, d2)
    nq = d2 + d4 + _dot_hi(d2, d4)
    nn = nq - d1 - _dot_hi(d1, nq)
    for s in (8, 16, 32):
        cpart = jnp.where(blk(2 * s) & jnp.logical_not(blk(s)), mm, 0.0)
        x = cpart + _dot_hi(cpart, nn)
        nn = nn - (x + _dot_hi(nn, x))

    rhs = jnp.concatenate([vb, kb * egc], axis=1)
    sol = rhs + _dot_hi(nn, rhs)
    u = sol[:, 0:DN_DV]
    w = sol[:, DN_DV:]

    qk = jnp.where(lower, _dot_nt(q, k) * decay, 0.0).astype(BF16)
    qg = q * egc

    vn_ref[...] = jnp.zeros_like(vn_ref)
    outs = []
    for c in range(tb // ck):
        lo, hi = c * ck, (c + 1) * ck
        s = s_ref[...]
        sb = s.astype(BF16)
        v_new = u[lo:hi] - jnp.dot(w[lo:hi].astype(BF16), sb, preferred_element_type=F32)
        vn_ref[lo:hi, :] = v_new
        o_c = jnp.dot(qg[lo:hi].astype(BF16), sb, preferred_element_type=F32)
        o_c += jnp.dot(qk[lo:hi, :], vn_ref[...].astype(BF16), preferred_element_type=F32)
        outs.append(o_c)
        g_last = gcol[hi - 1:hi, :]
        k_dec = k[lo:hi] * jnp.exp(g_last - gcol[lo:hi])
        s_ref[...] = s * jnp.exp(g_last[:, 0:1]) + _dot_tn(k_dec, v_new)
    o = jnp.concatenate(outs, axis=0)

    o = o * lax.rsqrt(jnp.mean(o * o, axis=-1, keepdims=True) + NORM_EPS) * nw_ref[...]
    o_ref[...] = (o * _silu(z_ref[...])).astype(o_ref.dtype)


def _gated_deltanet(proj_dn, small, small_t, conv_w, a_log, dt_bias, norm_w, bsz, seq):
    m = bsz * seq
    tb = DN_BLOCK
    nc = seq // tb
    hq = DN_HEADS
    tok = lambda off: pl.BlockSpec((tb, LANE), lambda b, h, c: (b * nc + c, off + h))
    cw = lambda off: pl.BlockSpec((CONV_K, LANE), lambda b, h, c: (0, off + h))
    smem = pl.BlockSpec(memory_space=pltpu.SMEM)
    return pl.pallas_call(
        _dn_kernel,
        grid=(bsz, hq, nc),
        in_specs=[
            tok(0), tok(hq), tok(2 * hq), tok(3 * hq),
            pl.BlockSpec((tb, SMALL_W), lambda b, h, c: (b * nc + c, 0)),
            pl.BlockSpec((None, SMALL_ROWS, tb), lambda b, h, c: (b, 0, c)),
            cw(0), cw(hq), cw(2 * hq),
            smem, smem,
            pl.BlockSpec((1, DN_DV), lambda b, h, c: (0, 0)),
        ],
        out_specs=pl.BlockSpec((tb, LANE), lambda b, h, c: (b * nc + c, h)),
        out_shape=jax.ShapeDtypeStruct((m, DN_WIDTH), BF16),
        scratch_shapes=[
            pltpu.VMEM((DN_DK, DN_DV), F32),
            pltpu.VMEM((3, tb + 8, LANE), F32),
            pltpu.VMEM((tb, DN_DV), F32),
        ],
        compiler_params=_cparams(3),
        name="gated_deltanet",
    )(proj_dn, proj_dn, proj_dn, proj_dn, small, small_t, conv_w, conv_w, conv_w,
      a_log, dt_bias, norm_w.reshape(1, DN_DV))


def _s5_prep_kernel(are_ref, aim_ref, ldt_ref, bre_ref, bim_ref, pr_ref, pi_ref, bbr_ref, bbi_ref):
    lr, li = are_ref[...], aim_ref[...]
    dt = jnp.exp(ldt_ref[...])
    mag = jnp.exp(lr * dt)
    ab_re, ab_im = mag * jnp.cos(li * dt), mag * jnp.sin(li * dt)
    den = lr * lr + li * li
    er, ei = ab_re - 1.0, ab_im
    fr = (lr * er + li * ei) / den
    fi = (lr * ei - li * er) / den
    for c in range(S5_GROUP):
        br, bi = bre_ref[c], bim_ref[c]
        bbr_ref[c] = fr * br - fi * bi
        bbi_ref[c] = fr * bi + fi * br
    pr, pi = ab_re, ab_im
    pr_ref[0], pi_ref[0] = pr, pi
    for n in range(1, S5_ROWS):
        pr, pi = pr * ab_re - pi * ab_im, pr * ab_im + pi * ab_re
        pr_ref[n], pi_ref[n] = pr, pi


def _s5_prep(a_re, a_im, log_dt, b_re, b_im):
    g, p = S5_GROUPS, S5_STATE
    shp = lambda n: jax.ShapeDtypeStruct((n, g, p), F32)
    return pl.pallas_call(
        _s5_prep_kernel,
        out_shape=(shp(S5_ROWS), shp(S5_ROWS), shp(S5_GROUP), shp(S5_GROUP)),
        name="s5_prep",
    )(a_re, a_im, log_dt.reshape(g, 1), jnp.transpose(b_re, (2, 0, 1)), jnp.transpose(b_im, (2, 0, 1)))


def _gelu_tanh(x):
    return 0.5 * x * (1.0 + jnp.tanh(np.sqrt(2.0 / np.pi).astype(np.float32) * (x + 0.044715 * (x * x * x))))


def _s5_kernel(u_ref, wbr_ref, wbi_ref, tr_ref, ti_ref, wcr_ref, wci_ref, d_ref, wglu_ref, bglu_ref,
               o_ref, xr_ref, xi_ref, cr_ref, ci_ref, y_ref):
    tm = S5_BLOCK
    kt = S5_TILE_G * S5_GROUP
    nt = S5_TILE_G * S5_STATE

    @pl.when(pl.program_id(1) == 0)
    def _():
        cr_ref[...] = jnp.zeros_like(cr_ref)
        ci_ref[...] = jnp.zeros_like(ci_ref)

    u = u_ref[...]
    ub = u.astype(BF16)
    for t in range(S5_GROUPS // S5_TILE_G):
        ut = ub[:, t * kt:(t + 1) * kt]
        xr_ref[:, t * nt:(t + 1) * nt] = jnp.dot(ut, wbr_ref[t], preferred_element_type=F32)
        xi_ref[:, t * nt:(t + 1) * nt] = jnp.dot(ut, wbi_ref[t], preferred_element_type=F32)

    def block(i, _):
        r0 = pl.multiple_of(i * S5_ROWS, S5_ROWS)
        for cb in range(S5_LANES // LANE):
            cs = slice(cb * LANE, (cb + 1) * LANE)
            xr = xr_ref[pl.ds(r0, S5_ROWS), cs]
            xi = xi_ref[pl.ds(r0, S5_ROWS), cs]
            for n, d in enumerate((1, 2, 4)):
                ar, ai = tr_ref[n, :, cs], ti_ref[n, :, cs]
                sr, si = pltpu.roll(xr, d, 0), pltpu.roll(xi, d, 0)
                xr, xi = xr + (ar * sr - ai * si), xi + (ar * si + ai * sr)
            pr, pi = tr_ref[3, :, cs], ti_ref[3, :, cs]
            cr, ci = cr_ref[:, cs], ci_ref[:, cs]
            xr, xi = xr + (pr * cr - pi * ci), xi + (pr * ci + pi * cr)
            xr_ref[pl.ds(r0, S5_ROWS), cs] = xr
            xi_ref[pl.ds(r0, S5_ROWS), cs] = xi
            cr_ref[:, cs] = jnp.broadcast_to(xr[S5_ROWS - 1:S5_ROWS, :], (S5_ROWS, LANE))
            ci_ref[:, cs] = jnp.broadcast_to(xi[S5_ROWS - 1:S5_ROWS, :], (S5_ROWS, LANE))
        return 0

    lax.fori_loop(0, tm // S5_ROWS, block, 0)

    for t in range(S5_GROUPS // S5_TILE_G):
        sr = xr_ref[:, t * nt:(t + 1) * nt].astype(BF16)
        si = xi_ref[:, t * nt:(t + 1) * nt].astype(BF16)
        yt = jnp.dot(sr, wcr_ref[t], preferred_element_type=F32)
        yt -= jnp.dot(si, wci_ref[t], preferred_element_type=F32)
        y_ref[:, t * kt:(t + 1) * kt] = yt + d_ref[:, t * kt:(t + 1) * kt] * u[:, t * kt:(t + 1) * kt]

    hg = jnp.dot(_gelu_tanh(y_ref[...]).astype(BF16), wglu_ref[...], preferred_element_type=F32)
    hg += bglu_ref[...]
    o_ref[...] = (hg[:, 0:S5_CH] * _sigmoid(hg[:, S5_CH:])).astype(o_ref.dtype)


def _s5_mixer(u, prep, c_re, c_im, d, w_glu, b_glu, bsz, seq):
    pw_r, pw_i, bb_r, bb_i = prep
    g, p, tg = S5_GROUPS, S5_STATE, S5_TILE_G
    nt = g // tg
    eye = jnp.eye(tg, dtype=F32)
    wb = lambda bb: jnp.einsum("ctip,ij->ticjp", bb.reshape(S5_GROUP, nt, tg, p), eye).reshape(
        nt, tg * S5_GROUP, tg * p).astype(BF16)
    wc = lambda cc: jnp.einsum("ticp,ij->tjpic", cc.reshape(nt, tg, S5_GROUP, p), eye).reshape(
        nt, tg * p, tg * S5_GROUP).astype(BF16)
    pr, pi = pw_r.reshape(S5_ROWS, S5_LANES), pw_i.reshape(S5_ROWS, S5_LANES)
    rows = jnp.arange(S5_ROWS)[:, None]
    tab = lambda pw: jnp.stack(
        [jnp.where(rows >= dd, pw[dd - 1][None, :], 0.0) for dd in (1, 2, 4)] + [pw])
    m = bsz * seq
    tm = S5_BLOCK
    nc = seq // tm
    full = lambda *shape: pl.BlockSpec(shape, lambda b, c: (0,) * len(shape))
    return pl.pallas_call(
        _s5_kernel,
        grid=(bsz, nc),
        in_specs=[
            pl.BlockSpec((tm, S5_CH), lambda b, c: (b * nc + c, 0)),
            full(nt, tg * S5_GROUP, tg * p), full(nt, tg * S5_GROUP, tg * p),
            full(4, S5_ROWS, S5_LANES), full(4, S5_ROWS, S5_LANES),
            full(nt, tg * p, tg * S5_GROUP), full(nt, tg * p, tg * S5_GROUP),
            full(1, S5_CH), full(S5_CH, 2 * S5_CH), full(1, 2 * S5_CH),
        ],
        out_specs=pl.BlockSpec((tm, S5_CH), lambda b, c: (b * nc + c, 0)),
        out_shape=jax.ShapeDtypeStruct((m, S5_CH), BF16),
        scratch_shapes=[
            pltpu.VMEM((tm, S5_LANES), F32), pltpu.VMEM((tm, S5_LANES), F32),
            pltpu.VMEM((S5_ROWS, S5_LANES), F32), pltpu.VMEM((S5_ROWS, S5_LANES), F32),
            pltpu.VMEM((tm, S5_CH), F32),
        ],
        compiler_params=_cparams(2),
        name="s5_mixer",
    )(u, wb(bb_r), wb(bb_i), tab(pr), tab(pi), wc(c_re), wc(c_im),
      d.reshape(1, S5_CH), w_glu.astype(BF16), b_glu.reshape(1, 2 * S5_CH))


def _m2_kernel(p_ref, sm_ref, smt_ref, cw_ref, cb_ref, arow_ref, brow_ref, acol_ref, bcol_ref,
               drow_ref, nw_ref, o_ref, s_ref, buf_ref, y_ref):
    t = M2_CHUNK
    n = M2_STATE
    hd = M2_HEADDIM

    @pl.when(pl.program_id(1) == 0)
    def _():
        s_ref[...] = jnp.zeros_like(s_ref)
        buf_ref[0:8, :] = jnp.zeros((8, M2_XBC), F32)

    z = p_ref[:, 0:M2_INNER]
    xbc = _silu(_causal_conv(buf_ref, p_ref[:, M2_INNER:], cw_ref[...], t) + cb_ref[...])
    xs = xbc[:, 0:M2_INNER]
    bm = xbc[:, M2_INNER:M2_INNER + M2_GROUPS * n]
    cm = xbc[:, M2_INNER + M2_GROUPS * n:]

    dt_c = _softplus(sm_ref[...] + brow_ref[...])
    adt_c = dt_c * -jnp.exp(arow_ref[...])
    dt_r = _softplus(smt_ref[2 * DN_HEADS:, :] + bcol_ref[...])
    adt_r = dt_r * -jnp.exp(acol_ref[...])

    row = lax.broadcasted_iota(jnp.int32, (t, t), 0)
    col = lax.broadcasted_iota(jnp.int32, (t, t), 1)
    lower = row >= col
    tri = jnp.where(lower, 1.0, 0.0).astype(BF16)
    tri_t = jnp.where(row <= col, 1.0, 0.0).astype(BF16)
    acs_c = _dot_exact_lhs(tri, adt_c)
    acs_r = _dot_exact_rhs(adt_r, tri_t)

    for g in range(M2_GROUPS):
        bg = bm[:, g * n:(g + 1) * n].astype(BF16)
        cg = cm[:, g * n:(g + 1) * n].astype(BF16)
        cbm = lax.dot_general(cg, bg, (((1,), (1,)), ((), ())), preferred_element_type=F32)
        for j in range(M2_HPG):
            hh = g * M2_HPG + j
            ln = 2 * DN_HEADS + hh
            a_col = acs_c[:, ln:ln + 1]
            a_row = acs_r[hh:hh + 1, :]
            seg = jnp.exp(jnp.where(lower, a_col - a_row, -1e30))
            xh = xs[:, hh * hd:(hh + 1) * hd]
            xdt = xh * dt_c[:, ln:ln + 1]
            s_h = s_ref[hh]
            y = jnp.dot((cbm * seg).astype(BF16), xdt.astype(BF16), preferred_element_type=F32)
            y += lax.dot_general(cg, s_h.astype(BF16), (((1,), (1,)), ((), ())),
                                 preferred_element_type=F32) * jnp.exp(a_col)
            y += drow_ref[:, ln:ln + 1] * xh
            y_ref[:, hh * hd:(hh + 1) * hd] = y
            a_last = a_col[t - 1:t, :]
            xdec = xdt * jnp.exp(a_last - a_col)
            s_ref[hh] = s_h * jnp.exp(a_last) + lax.dot_general(
                xdec.astype(BF16), bg, (((0,), (0,)), ((), ())), preferred_element_type=F32)

    gw = M2_INNER // M2_GROUPS
    for g in range(M2_GROUPS):
        sl = slice(g * gw, (g + 1) * gw)
        y = y_ref[:, sl] * _silu(z[:, sl])
        y = y * lax.rsqrt(jnp.mean(y * y, axis=-1, keepdims=True) + NORM_EPS) * nw_ref[:, sl]
        o_ref[:, sl] = y.astype(o_ref.dtype)


def _mamba2(proj_m2, small, small_t, conv_w, conv_b, a_log, dt_bias, d, norm_w, bsz, seq):
    m = bsz * seq
    t = M2_CHUNK
    nc = seq // t
    lo = 2 * DN_HEADS
    lane_pad = lambda v: jnp.zeros((1, SMALL_W), F32).at[0, lo:lo + M2_HEADS].set(v.astype(F32))
    full = lambda *shape: pl.BlockSpec(shape, lambda b, c: (0,) * len(shape))
    return pl.pallas_call(
        _m2_kernel,
        grid=(bsz, nc),
        in_specs=[
            pl.BlockSpec((t, M2_INNER + M2_XBC), lambda b, c: (b * nc + c, 0)),
            pl.BlockSpec((t, SMALL_W), lambda b, c: (b * nc + c, 0)),
            pl.BlockSpec((None, SMALL_ROWS, t), lambda b, c: (b, 0, c)),
            full(CONV_K, M2_XBC), full(1, M2_XBC),
            full(1, SMALL_W), full(1, SMALL_W), full(M2_HEADS, 1), full(M2_HEADS, 1),
            full(1, SMALL_W), full(1, M2_INNER),
        ],
        out_specs=pl.BlockSpec((t, M2_INNER), lambda b, c: (b * nc + c, 0)),
        out_shape=jax.ShapeDtypeStruct((m, M2_INNER), BF16),
        scratch_shapes=[
            pltpu.VMEM((M2_HEADS, M2_HEADDIM, M2_STATE), F32),
            pltpu.VMEM((t + 8, M2_XBC), F32),
            pltpu.VMEM((t, M2_INNER), F32),
        ],
        compiler_params=_cparams(2),
        name="mamba2_ssd",
    )(proj_m2, small, small_t, conv_w, conv_b.reshape(1, M2_XBC),
      lane_pad(a_log), lane_pad(dt_bias), a_log.astype(F32).reshape(M2_HEADS, 1),
      dt_bias.astype(F32).reshape(M2_HEADS, 1), lane_pad(d), norm_w.reshape(1, M2_INNER))


def _split_w_in(w):
    o = np.cumsum((0,) + IN_SIZES)
    sl = lambda i: w[:, o[i]:o[i + 1]]
    w_dn = jnp.concatenate([sl(0), sl(1)], axis=1).astype(BF16)
    w_s5 = sl(4).astype(BF16)
    w_m2 = jnp.concatenate([sl(5), sl(6)], axis=1).astype(BF16)
    pad = jnp.zeros((w.shape[0], SMALL_W - 2 * DN_HEADS - M2_HEADS), w.dtype)
    w_sm = jnp.concatenate([sl(2), sl(3), sl(7), pad], axis=1).astype(BF16)
    return w_dn, w_s5, w_m2, w_sm


def kernel(x, c, w_mod, b_mod, mod_table, norm_ffn1, norm_mix, norm_ffn2, ffn1_w_gate, ffn1_w_up, ffn1_w_down, ffn2_w_gate, ffn2_w_up, ffn2_w_down, w_in, w_out, dn_conv_w, dn_a_log, dn_dt_bias, dn_norm_w, s5_a_re, s5_a_im, s5_b_re, s5_b_im, s5_c_re, s5_c_im, s5_log_dt, s5_d, s5_w_glu, s5_b_glu, m2_conv_w, m2_conv_b, m2_a_log, m2_dt_bias, m2_d, m2_norm_w, final_norm):
    bsz, seq, _ = x.shape
    m = bsz * seq
    mod = _modulation(c, w_mod, b_mod, mod_table)
    xf = x.reshape(m, D_MODEL).astype(F32)

    def ffn(xf, modl, norm_w, wg, wu, wd, k0):
        h = _norm_modulate(xf, norm_w, modl, k0, k0 + 1, seq)
        a = _ffn_gateup(h, wg.astype(BF16), wu.astype(BF16))
        return _ffn_down(a, wd.astype(BF16), xf, modl, k0 + 2, seq)

    for i in range(DEPTH):
        modl = mod[i]
        xf = ffn(xf, modl, norm_ffn1[i], ffn1_w_gate[i], ffn1_w_up[i], ffn1_w_down[i], 0)

        h = _norm_modulate(xf, norm_mix[i], modl, 3, 4, seq)
        w_dn, w_s5, w_m2, w_sm = _split_w_in(w_in[i])
        proj_dn = _matmul(h, w_dn, 1024, name="in_proj_dn")
        proj_s5 = _matmul(h, w_s5, 1024, name="in_proj_s5")
        proj_m2 = _matmul(h, w_m2, 1024, name="in_proj_m2")
        small = _matmul(h, w_sm, SMALL_W, name="in_proj_small")
        small_t = jnp.transpose(small.reshape(bsz, seq, SMALL_W)[:, :, :SMALL_ROWS], (0, 2, 1))

        y_dn = _gated_deltanet(proj_dn, small, small_t, dn_conv_w[i], dn_a_log[i].astype(F32),
                               dn_dt_bias[i].astype(F32), dn_norm_w[i], bsz, seq)
        prep = _s5_prep(s5_a_re[i], s5_a_im[i], s5_log_dt[i], s5_b_re[i], s5_b_im[i])
        y_s5 = _s5_mixer(proj_s5, prep, s5_c_re[i], s5_c_im[i], s5_d[i], s5_w_glu[i], s5_b_glu[i],
                         bsz, seq)
        y_m2 = _mamba2(proj_m2, small, small_t, m2_conv_w[i], m2_conv_b[i], m2_a_log[i],
                       m2_dt_bias[i], m2_d[i], m2_norm_w[i], bsz, seq)
        xf = _out_proj(y_dn, y_s5, y_m2, w_out[i].astype(BF16), xf, modl, 5, seq)

        xf = ffn(xf, modl, norm_ffn2[i], ffn2_w_gate[i], ffn2_w_up[i], ffn2_w_down[i], 6)

    return _final_norm(xf, final_norm).reshape(bsz, seq, D_MODEL)
```

```python
import functools

import jax
import jax.numpy as jnp
import numpy as np
from jax import lax
from jax.experimental import pallas as pl
from jax.experimental.pallas import tpu as pltpu

F32 = jnp.float32
BF16 = jnp.bfloat16

D_MODEL = 4096
DEPTH = 2
D_FF = 11008
FFN_RES = 0.5
N_MOD = 9
NORM_EPS = 1e-6
CONV_K = 4

DN_HEADS = 12
DN_DK = 128
DN_DV = 128
DN_BLOCK = 256
DN_BASE = 8
DN_HPS = 4

S5_CH = 1024
S5_GROUP = 16
S5_GROUPS = 64
S5_STATE = 64
S5_LANES = S5_GROUPS * S5_STATE
S5_TILE_G = 8
S5_ROWS = 8
S5_BLOCK = 256

M2_HEADS = 24
M2_HEADDIM = 64
M2_INNER = M2_HEADS * M2_HEADDIM
M2_GROUPS = 4
M2_HPG = M2_HEADS // M2_GROUPS
M2_STATE = 128
M2_CHUNK = 128

DN_WIDTH = DN_HEADS * DN_DV
DN_QKV = 2 * DN_HEADS * DN_DK + DN_HEADS * DN_DV
M2_XBC = M2_INNER + 2 * M2_GROUPS * M2_STATE
IN_SIZES = (DN_QKV, DN_WIDTH, DN_HEADS, DN_HEADS, S5_CH, M2_INNER, M2_XBC, M2_HEADS)
SMALL_W = 128
SMALL_ROWS = 48

LANE = 128
MM_ROWS = 1024
VMEM_LIMIT = 56 * 1024 * 1024


def _cparams(n_axes):
    return pltpu.CompilerParams(
        dimension_semantics=("arbitrary",) * n_axes, vmem_limit_bytes=VMEM_LIMIT)


def _sigmoid(x):
    return 1.0 / (1.0 + jnp.exp(-x))


def _silu(x):
    return x * _sigmoid(x)


def _softplus(x):
    return jnp.maximum(x, 0.0) + jnp.log(1.0 + jnp.exp(-jnp.abs(x)))


def _dot(a, b):
    return jnp.dot(a.astype(BF16), b.astype(BF16), preferred_element_type=F32)


def _dot_nt(a, b):
    return lax.dot_general(a.astype(BF16), b.astype(BF16), (((1,), (1,)), ((), ())),
                           preferred_element_type=F32)


def _dot_tn(a, b):
    return lax.dot_general(a.astype(BF16), b.astype(BF16), (((0,), (0,)), ((), ())),
                           preferred_element_type=F32)


def _split3(x):
    x1 = x.astype(BF16)
    r = x - x1.astype(F32)
    x2 = r.astype(BF16)
    x3 = (r - x2.astype(F32)).astype(BF16)
    return x1, x2, x3


def _dot_exact_lhs(mask_bf16, x):
    x1, x2, x3 = _split3(x)
    d = lambda t: jnp.dot(mask_bf16, t, preferred_element_type=F32)
    return d(x1) + d(x2) + d(x3)


def _dot_exact_rhs(x, mask_bf16):
    x1, x2, x3 = _split3(x)
    d = lambda t: jnp.dot(t, mask_bf16, preferred_element_type=F32)
    return d(x1) + d(x2) + d(x3)


def _dot_hi(a, b):
    a1 = a.astype(BF16)
    a2 = (a - a1.astype(F32)).astype(BF16)
    b1 = b.astype(BF16)
    b2 = (b - b1.astype(F32)).astype(BF16)
    d = lambda s, t: jnp.dot(s, t, preferred_element_type=F32)
    return d(a1, b1) + (d(a1, b2) + d(a2, b1))


def _mod_kernel(c_ref, w_ref, b_ref, t_ref, o_ref):
    acc = _dot(_silu(c_ref[...]), w_ref[...]) + b_ref[...]
    for l in range(DEPTH):
        o_ref[l] = acc + t_ref[l]


def _modulation(c, w_mod, b_mod, mod_table):
    bsz = c.shape[0]
    rows = 8
    c_pad = jnp.zeros((rows, D_MODEL), F32).at[:bsz].set(c.astype(F32))
    n = N_MOD * D_MODEL
    tn = 512
    out = pl.pallas_call(
        _mod_kernel,
        grid=(n // tn,),
        in_specs=[
            pl.BlockSpec((rows, D_MODEL), lambda j: (0, 0)),
            pl.BlockSpec((D_MODEL, tn), lambda j: (0, j)),
            pl.BlockSpec((1, tn), lambda j: (0, j)),
            pl.BlockSpec((DEPTH, 1, tn), lambda j: (0, 0, j)),
        ],
        out_specs=pl.BlockSpec((DEPTH, rows, tn), lambda j: (0, 0, j)),
        out_shape=jax.ShapeDtypeStruct((DEPTH, rows, n), F32),
        compiler_params=_cparams(1),
        name="adaln_mod",
    )(c_pad, w_mod, b_mod.reshape(1, n), mod_table.reshape(DEPTH, 1, n))
    return out[:, :bsz].reshape(DEPTH, bsz * N_MOD, 1, D_MODEL)


def _normmod_kernel(x_ref, w_ref, sh_ref, sc_ref, o_ref):
    x = x_ref[...]
    y = x * lax.rsqrt(jnp.mean(x * x, axis=-1, keepdims=True) + NORM_EPS) * w_ref[...]
    o_ref[...] = (y * (1.0 + sc_ref[...]) + sh_ref[...]).astype(o_ref.dtype)


def _norm_kernel(x_ref, w_ref, o_ref):
    x = x_ref[...]
    y = x * lax.rsqrt(jnp.mean(x * x, axis=-1, keepdims=True) + NORM_EPS) * w_ref[...]
    o_ref[...] = y.astype(o_ref.dtype)


def _norm_modulate(x, w, modl, k_shift, k_scale, seq):
    m = x.shape[0]
    tm = 256
    per_b = seq // tm
    vec = lambda k: pl.BlockSpec((None, 1, D_MODEL), lambda i: ((i // per_b) * N_MOD + k, 0, 0))
    return pl.pallas_call(
        _normmod_kernel,
        grid=(m // tm,),
        in_specs=[
            pl.BlockSpec((tm, D_MODEL), lambda i: (i, 0)),
            pl.BlockSpec((1, D_MODEL), lambda i: (0, 0)),
            vec(k_shift),
            vec(k_scale),
        ],
        out_specs=pl.BlockSpec((tm, D_MODEL), lambda i: (i, 0)),
        out_shape=jax.ShapeDtypeStruct((m, D_MODEL), BF16),
        compiler_params=_cparams(1),
        name="norm_modulate",
    )(x, w.reshape(1, D_MODEL), modl, modl)


def _final_norm(x, w):
    m = x.shape[0]
    tm = 256
    return pl.pallas_call(
        _norm_kernel,
        grid=(m // tm,),
        in_specs=[
            pl.BlockSpec((tm, D_MODEL), lambda i: (i, 0)),
            pl.BlockSpec((1, D_MODEL), lambda i: (0, 0)),
        ],
        out_specs=pl.BlockSpec((tm, D_MODEL), lambda i: (i, 0)),
        out_shape=jax.ShapeDtypeStruct((m, D_MODEL), F32),
        compiler_params=_cparams(1),
        name="final_norm",
    )(x, w.reshape(1, D_MODEL))


def _first_row_tile():
    return pl.program_id(1) == 0


def _mm_kernel(a_ref, w_ref, o_ref, wb_ref):
    @pl.when(_first_row_tile())
    def _():
        wb_ref[...] = w_ref[...].astype(BF16)

    o_ref[...] = jnp.dot(a_ref[...], wb_ref[...], preferred_element_type=F32).astype(o_ref.dtype)


def _matmul(a, w, tn, col0, n, name):
    m, k = a.shape
    tm = min(m, MM_ROWS)
    j0 = col0 // tn
    return pl.pallas_call(
        _mm_kernel,
        grid=(n // tn, m // tm),
        in_specs=[
            pl.BlockSpec((tm, k), lambda j, i: (i, 0)),
            pl.BlockSpec((k, tn), lambda j, i: (0, j0 + j)),
        ],
        out_specs=pl.BlockSpec((tm, tn), lambda j, i: (i, j)),
        out_shape=jax.ShapeDtypeStruct((m, n), F32),
        scratch_shapes=[pltpu.VMEM((k, tn), BF16)],
        compiler_params=_cparams(2),
        name=name,
    )(a, w)


def _small_kernel(a_ref, wa_ref, wb_ref, o_ref):
    a = a_ref[...]
    lane = lax.broadcasted_iota(jnp.int32, (1, SMALL_W), 1)
    ya = _dot(a, wa_ref[...])
    yb = _dot(a, wb_ref[...])
    y = jnp.where(lane < 2 * DN_HEADS, ya, yb)
    o_ref[...] = jnp.where(lane < SMALL_ROWS, y, 0.0)


def _in_proj_small(a, w):
    m, k = a.shape
    o = np.cumsum((0,) + IN_SIZES)
    assert o[2] % LANE == 0 and o[7] % LANE == 2 * DN_HEADS
    tm = min(m, MM_ROWS)
    return pl.pallas_call(
        _small_kernel,
        grid=(m // tm,),
        in_specs=[
            pl.BlockSpec((tm, k), lambda i: (i, 0)),
            pl.BlockSpec((k, LANE), lambda i: (0, int(o[2]) // LANE)),
            pl.BlockSpec((k, LANE), lambda i: (0, int(o[7]) // LANE)),
        ],
        out_specs=pl.BlockSpec((tm, SMALL_W), lambda i: (i, 0)),
        out_shape=jax.ShapeDtypeStruct((m, SMALL_W), F32),
        compiler_params=_cparams(1),
        name="in_proj_small",
    )(a, w, w)


def _gateup_kernel(h_ref, wg_ref, wu_ref, o_ref, wgb_ref, wub_ref):
    @pl.when(_first_row_tile())
    def _():
        wgb_ref[...] = wg_ref[...].astype(BF16)
        wub_ref[...] = wu_ref[...].astype(BF16)

    h = h_ref[...]
    g = jnp.dot(h, wgb_ref[...], preferred_element_type=F32)
    u = jnp.dot(h, wub_ref[...], preferred_element_type=F32)
    o_ref[...] = (_silu(g) * u).astype(o_ref.dtype)


def _ffn_gateup(h, wg, wu):
    m = h.shape[0]
    tm = min(m, MM_ROWS)
    tn = 256
    return pl.pallas_call(
        _gateup_kernel,
        grid=(D_FF // tn, m // tm),
        in_specs=[
            pl.BlockSpec((tm, D_MODEL), lambda j, i: (i, 0)),
            pl.BlockSpec((D_MODEL, tn), lambda j, i: (0, j)),
            pl.BlockSpec((D_MODEL, tn), lambda j, i: (0, j)),
        ],
        out_specs=pl.BlockSpec((tm, tn), lambda j, i: (i, j)),
        out_shape=jax.ShapeDtypeStruct((m, D_FF), BF16),
        scratch_shapes=[pltpu.VMEM((D_MODEL, tn), BF16), pltpu.VMEM((D_MODEL, tn), BF16)],
        compiler_params=_cparams(2),
        name="ffn_gateup",
    )(h, wg, wu)


def _down_kernel(a_ref, w_ref, x_ref, g_ref, o_ref, acc_ref):
    k = pl.program_id(2)

    @pl.when(k == 0)
    def _():
        acc_ref[...] = jnp.zeros_like(acc_ref)

    acc_ref[...] += jnp.dot(a_ref[...], w_ref[...], preferred_element_type=F32)

    @pl.when(k == pl.num_programs(2) - 1)
    def _():
        o_ref[...] = x_ref[...] + FFN_RES * g_ref[...] * acc_ref[...]


def _ffn_down(a, wd, x, modl, k_gate, seq):
    m = a.shape[0]
    tm = min(m, MM_ROWS)
    tn = 512
    tk = D_FF // 2
    per_b = seq // tm
    return pl.pallas_call(
        _down_kernel,
        grid=(m // tm, D_MODEL // tn, D_FF // tk),
        in_specs=[
            pl.BlockSpec((tm, tk), lambda i, j, k: (i, k)),
            pl.BlockSpec((tk, tn), lambda i, j, k: (k, j)),
            pl.BlockSpec((tm, tn), lambda i, j, k: (i, j)),
            pl.BlockSpec((None, 1, tn), lambda i, j, k: ((i // per_b) * N_MOD + k_gate, 0, j)),
        ],
        out_specs=pl.BlockSpec((tm, tn), lambda i, j, k: (i, j)),
        out_shape=jax.ShapeDtypeStruct((m, D_MODEL), F32),
        scratch_shapes=[pltpu.VMEM((tm, tn), F32)],
        compiler_params=_cparams(3),
        name="ffn_down",
    )(a, wd, x, modl)


def _outproj_kernel(ydn_ref, ys5_ref, ym2_ref, w_ref, x_ref, g_ref, o_ref, wb_ref):
    @pl.when(_first_row_tile())
    def _():
        wb_ref[...] = w_ref[...].astype(BF16)

    acc = jnp.dot(ydn_ref[...], wb_ref[0:DN_WIDTH, :], preferred_element_type=F32)
    acc += jnp.dot(ys5_ref[...], wb_ref[DN_WIDTH:DN_WIDTH + S5_CH, :], preferred_element_type=F32)
    acc += jnp.dot(ym2_ref[...], wb_ref[DN_WIDTH + S5_CH:, :], preferred_element_type=F32)
    o_ref[...] = x_ref[...] + g_ref[...] * acc


def _out_proj(y_dn, y_s5, y_m2, w, x, modl, k_gate, seq):
    m = x.shape[0]
    tm = min(m, MM_ROWS)
    tn = 512
    per_b = seq // tm
    return pl.pallas_call(
        _outproj_kernel,
        grid=(D_MODEL // tn, m // tm),
        in_specs=[
            pl.BlockSpec((tm, DN_WIDTH), lambda j, i: (i, 0)),
            pl.BlockSpec((tm, S5_CH), lambda j, i: (i, 0)),
            pl.BlockSpec((tm, M2_INNER), lambda j, i: (i, 0)),
            pl.BlockSpec((D_MODEL, tn), lambda j, i: (0, j)),
            pl.BlockSpec((tm, tn), lambda j, i: (i, j)),
            pl.BlockSpec((None, 1, tn), lambda j, i: ((i // per_b) * N_MOD + k_gate, 0, j)),
        ],
        out_specs=pl.BlockSpec((tm, tn), lambda j, i: (i, j)),
        out_shape=jax.ShapeDtypeStruct((m, D_MODEL), F32),
        scratch_shapes=[pltpu.VMEM((D_MODEL, tn), BF16)],
        compiler_params=_cparams(2),
        name="out_proj",
    )(y_dn, y_s5, y_m2, w, x, modl)


def _causal_conv(buf_ref, raw, w, t):
    buf_ref[8:t + 8, :] = raw
    acc = raw * w[CONV_K - 1:CONV_K, :]
    for j in range(CONV_K - 1):
        off = 8 - (CONV_K - 1) + j
        acc += buf_ref[off:off + t, :] * w[j:j + 1, :]
    buf_ref[0:8, :] = raw[t - 8:t, :]
    return acc


def _dn_consts():
    idx = np.arange(DN_BLOCK)
    row, col = idx[:, None], idx[None, :]
    blk = lambda s: (row // s) == (col // s)
    strict = row > col
    masks = [blk(DN_BASE) & strict]
    s = DN_BASE
    while s < DN_BLOCK:
        masks.append(blk(2 * s) & ~blk(s) & strict)
        s *= 2
    tri = row >= col
    return (jnp.asarray(tri, BF16), jnp.asarray(tri.T, BF16),
            jnp.asarray(np.stack(masks), F32))


def _dn_kernel(q_ref, k_ref, v_ref, z_ref, sm_ref, smt_ref, wq_ref, wk_ref, wv_ref,
               alog_ref, dtb_ref, nw_ref, tri_ref, trit_ref, mask_ref, o_ref, s_ref, buf_ref):
    tb = DN_BLOCK

    @pl.when(pl.program_id(2) == 0)
    def _():
        s_ref[...] = jnp.zeros_like(s_ref)
        buf_ref[:, 0:8, :] = jnp.zeros((3, 8, DN_HPS * LANE), F32)

    q_all = _silu(_causal_conv(buf_ref.at[0], q_ref[...], wq_ref[...], tb))
    k_all = _silu(_causal_conv(buf_ref.at[1], k_ref[...], wk_ref[...], tb))
    v_all = _silu(_causal_conv(buf_ref.at[2], v_ref[...], wv_ref[...], tb))

    lane = lax.broadcasted_iota(jnp.int32, (1, SMALL_W), 1)
    sm = sm_ref[...]
    pick = lambda idx: jnp.sum(jnp.where(lane == idx, sm, 0.0), axis=1, keepdims=True)
    row = lax.broadcasted_iota(jnp.int32, (tb, tb), 0)
    col = lax.broadcasted_iota(jnp.int32, (tb, tb), 1)
    lower = row >= col
    one = jnp.ones((1, 1), F32)
    n_levels = mask_ref.shape[0]

    def head(i):
        h = pl.program_id(1) * DN_HPS + i
        sl = slice(i * LANE, (i + 1) * LANE)
        q, k, v = q_all[:, sl], k_all[:, sl], v_all[:, sl]
        q = q * lax.rsqrt(jnp.sum(q * q, axis=-1, keepdims=True) + 1e-6) * (DN_DK ** -0.5)
        k = k * lax.rsqrt(jnp.sum(k * k, axis=-1, keepdims=True) + 1e-6)

        coef = -jnp.exp(one * alog_ref[h])
        dtb = one * dtb_ref[h]
        g_c = coef * _softplus(pick(h) + dtb)
        g_r = coef * _softplus(smt_ref[pl.ds(h, 1), :] + dtb)
        beta_c = _sigmoid(pick(DN_HEADS + h))

        gcol = _dot_exact_lhs(tri_ref[...], jnp.broadcast_to(g_c, (tb, LANE)))
        gc_r = _dot_exact_rhs(jnp.broadcast_to(g_r, (8, tb)), trit_ref[...])
        yield
        gc_cc = jnp.concatenate([gcol] * (tb // LANE), axis=1)
        gc_rr = jnp.broadcast_to(gc_r[0:1, :], (tb, tb))
        decay = jnp.exp(jnp.where(lower, gc_cc - gc_rr, -1e30))
        egc = jnp.exp(gcol)

        kb = k * beta_c
        vb = v * beta_c
        kd = _dot_nt(kb, k) * decay
        qk = _dot_nt(q, k) * decay
        yield

        d1 = kd * mask_ref[0]
        d2 = _dot(d1, d1)
        yield
        d4 = _dot(d2, d2)
        yield
        nq = d2 + d4 + _dot(d2, d4)
        yield
        nn = nq - d1 - _dot(d1, nq)
        yield
        for lvl in range(1, n_levels):
            cpart = kd * mask_ref[lvl]
            x = cpart + _dot(cpart, nn)
            yield
            nn = nn - (x + _dot(nn, x))
            yield

        rhs = jnp.concatenate([vb, kb * egc], axis=1)
        sol = rhs + _dot(nn, rhs)
        yield
        u = sol[:, 0:DN_DV]
        w = sol[:, DN_DV:]

        s = s_ref[i]
        sb = s.astype(BF16)
        v_new = u - _dot(w, sb)
        o = _dot(q * egc, sb)
        yield
        o += _dot(qk, v_new)
        g_last = gcol[tb - 1:tb, :]
        s_ref[i] = s * jnp.exp(g_last[:, 0:1]) + _dot_tn(k * jnp.exp(g_last - gcol), v_new)
        yield

        o = o * lax.rsqrt(jnp.mean(o * o, axis=-1, keepdims=True) + NORM_EPS) * nw_ref[...]
        o_ref[:, sl] = (o * _silu(z_ref[:, sl])).astype(o_ref.dtype)

    done = object()
    chains = [head(i) for i in range(DN_HPS)]
    while chains:
        chains = [ch for ch in chains if next(ch, done) is not done]


def _gated_deltanet(proj_dn, small, small_t, conv_w, a_log, dt_bias, norm_w, bsz, seq):
    m = bsz * seq
    tb = DN_BLOCK
    nc = seq // tb
    hg = DN_HEADS // DN_HPS
    wd = DN_HPS * LANE
    tri, tri_t, masks = _dn_consts()
    tok = lambda off: pl.BlockSpec((tb, wd), lambda b, h, c: (b * nc + c, off * hg + h))
    cw = lambda off: pl.BlockSpec((CONV_K, wd), lambda b, h, c: (0, off * hg + h))
    smem = pl.BlockSpec(memory_space=pltpu.SMEM)
    full = lambda *shape: pl.BlockSpec(shape, lambda b, h, c: (0,) * len(shape))
    return pl.pallas_call(
        _dn_kernel,
        grid=(bsz, hg, nc),
        in_specs=[
            tok(0), tok(1), tok(2), tok(3),
            pl.BlockSpec((tb, SMALL_W), lambda b, h, c: (b * nc + c, 0)),
            pl.BlockSpec((None, SMALL_ROWS, tb), lambda b, h, c: (b, 0, c)),
            cw(0), cw(1), cw(2),
            smem, smem,
            full(1, DN_DV), full(tb, tb), full(tb, tb), full(*masks.shape),
        ],
        out_specs=pl.BlockSpec((tb, wd), lambda b, h, c: (b * nc + c, h)),
        out_shape=jax.ShapeDtypeStruct((m, DN_WIDTH), BF16),
        scratch_shapes=[
            pltpu.VMEM((DN_HPS, DN_DK, DN_DV), F32),
            pltpu.VMEM((3, tb + 8, wd), F32),
        ],
        compiler_params=_cparams(3),
        name="gated_deltanet",
    )(proj_dn, proj_dn, proj_dn, proj_dn, small, small_t, conv_w, conv_w, conv_w,
      a_log, dt_bias, norm_w.reshape(1, DN_DV), tri, tri_t, masks)


def _s5_prep_kernel(are_ref, aim_ref, ldt_ref, bre_ref, bim_ref, pr_ref, pi_ref, bbr_ref, bbi_ref):
    lr, li = are_ref[...], aim_ref[...]
    dt = jnp.exp(ldt_ref[...])
    mag = jnp.exp(lr * dt)
    ab_re, ab_im = mag * jnp.cos(li * dt), mag * jnp.sin(li * dt)
    den = lr * lr + li * li
    er, ei = ab_re - 1.0, ab_im
    fr = (lr * er + li * ei) / den
    fi = (lr * ei - li * er) / den
    for c in range(S5_GROUP):
        br, bi = bre_ref[c], bim_ref[c]
        bbr_ref[c] = fr * br - fi * bi
        bbi_ref[c] = fr * bi + fi * br
    pr, pi = ab_re, ab_im
    pr_ref[0], pi_ref[0] = pr, pi
    for n in range(1, S5_ROWS):
        pr, pi = pr * ab_re - pi * ab_im, pr * ab_im + pi * ab_re
        pr_ref[n], pi_ref[n] = pr, pi


def _s5_prep(a_re, a_im, log_dt, b_re, b_im):
    g, p = S5_GROUPS, S5_STATE
    shp = lambda n: jax.ShapeDtypeStruct((n, g, p), F32)
    return pl.pallas_call(
        _s5_prep_kernel,
        out_shape=(shp(S5_ROWS), shp(S5_ROWS), shp(S5_GROUP), shp(S5_GROUP)),
        name="s5_prep",
    )(a_re, a_im, log_dt.reshape(g, 1), jnp.transpose(b_re, (2, 0, 1)), jnp.transpose(b_im, (2, 0, 1)))


def _gelu_tanh(x):
    return 0.5 * x * (1.0 + jnp.tanh(np.sqrt(2.0 / np.pi).astype(np.float32) * (x + 0.044715 * (x * x * x))))


def _s5_kernel(u_ref, wbr_ref, wbi_ref, tr_ref, ti_ref, wcr_ref, wci_ref, d_ref, wglu_ref, bglu_ref,
               o_ref, xr_ref, xi_ref, cr_ref, ci_ref, y_ref):
    tm = S5_BLOCK
    kt = S5_TILE_G * S5_GROUP
    nt = S5_TILE_G * S5_STATE

    @pl.when(pl.program_id(1) == 0)
    def _():
        cr_ref[...] = jnp.zeros_like(cr_ref)
        ci_ref[...] = jnp.zeros_like(ci_ref)

    u = u_ref[...]
    ub = u.astype(BF16)
    for t in range(S5_GROUPS // S5_TILE_G):
        ut = ub[:, t * kt:(t + 1) * kt]
        xr_ref[:, t * nt:(t + 1) * nt] = jnp.dot(ut, wbr_ref[t], preferred_element_type=F32)
        xi_ref[:, t * nt:(t + 1) * nt] = jnp.dot(ut, wbi_ref[t], preferred_element_type=F32)

    def block(i, _):
        r0 = pl.multiple_of(i * S5_ROWS, S5_ROWS)
        for cb in range(S5_LANES // LANE):
            cs = slice(cb * LANE, (cb + 1) * LANE)
            xr = xr_ref[pl.ds(r0, S5_ROWS), cs]
            xi = xi_ref[pl.ds(r0, S5_ROWS), cs]
            for n, d in enumerate((1, 2, 4)):
                ar, ai = tr_ref[n, :, cs], ti_ref[n, :, cs]
                sr, si = pltpu.roll(xr, d, 0), pltpu.roll(xi, d, 0)
                xr, xi = xr + (ar * sr - ai * si), xi + (ar * si + ai * sr)
            pr, pi = tr_ref[3, :, cs], ti_ref[3, :, cs]
            cr, ci = cr_ref[:, cs], ci_ref[:, cs]
            xr, xi = xr + (pr * cr - pi * ci), xi + (pr * ci + pi * cr)
            xr_ref[pl.ds(r0, S5_ROWS), cs] = xr
            xi_ref[pl.ds(r0, S5_ROWS), cs] = xi
            cr_ref[:, cs] = jnp.broadcast_to(xr[S5_ROWS - 1:S5_ROWS, :], (S5_ROWS, LANE))
            ci_ref[:, cs] = jnp.broadcast_to(xi[S5_ROWS - 1:S5_ROWS, :], (S5_ROWS, LANE))
        return 0

    lax.fori_loop(0, tm // S5_ROWS, block, 0)

    for t in range(S5_GROUPS // S5_TILE_G):
        sr = xr_ref[:, t * nt:(t + 1) * nt].astype(BF16)
        si = xi_ref[:, t * nt:(t + 1) * nt].astype(BF16)
        yt = jnp.dot(sr, wcr_ref[t], preferred_element_type=F32)
        yt -= jnp.dot(si, wci_ref[t], preferred_element_type=F32)
        y_ref[:, t * kt:(t + 1) * kt] = yt + d_ref[:, t * kt:(t + 1) * kt] * u[:, t * kt:(t + 1) * kt]

    hg = jnp.dot(_gelu_tanh(y_ref[...]).astype(BF16), wglu_ref[...], preferred_element_type=F32)
    hg += bglu_ref[...]
    o_ref[...] = (hg[:, 0:S5_CH] * _sigmoid(hg[:, S5_CH:])).astype(o_ref.dtype)


def _s5_mixer(u, prep, c_re, c_im, d, w_glu, b_glu, bsz, seq):
    pw_r, pw_i, bb_r, bb_i = prep
    g, p, tg = S5_GROUPS, S5_STATE, S5_TILE_G
    nt = g // tg
    eye = jnp.eye(tg, dtype=F32)
    wb = lambda bb: jnp.einsum("ctip,ij->ticjp", bb.reshape(S5_GROUP, nt, tg, p), eye).reshape(
        nt, tg * S5_GROUP, tg * p).astype(BF16)
    wc = lambda cc: jnp.einsum("ticp,ij->tjpic", cc.reshape(nt, tg, S5_GROUP, p), eye).reshape(
        nt, tg * p, tg * S5_GROUP).astype(BF16)
    pr, pi = pw_r.reshape(S5_ROWS, S5_LANES), pw_i.reshape(S5_ROWS, S5_LANES)
    rows = jnp.arange(S5_ROWS)[:, None]
    tab = lambda pw: jnp.stack(
        [jnp.where(rows >= dd, pw[dd - 1][None, :], 0.0) for dd in (1, 2, 4)] + [pw])
    m = bsz * seq
    tm = S5_BLOCK
    nc = seq // tm
    full = lambda *shape: pl.BlockSpec(shape, lambda b, c: (0,) * len(shape))
    return pl.pallas_call(
        _s5_kernel,
        grid=(bsz, nc),
        in_specs=[
            pl.BlockSpec((tm, S5_CH), lambda b, c: (b * nc + c, 0)),
            full(nt, tg * S5_GROUP, tg * p), full(nt, tg * S5_GROUP, tg * p),
            full(4, S5_ROWS, S5_LANES), full(4, S5_ROWS, S5_LANES),
            full(nt, tg * p, tg * S5_GROUP), full(nt, tg * p, tg * S5_GROUP),
            full(1, S5_CH), full(S5_CH, 2 * S5_CH), full(1, 2 * S5_CH),
        ],
        out_specs=pl.BlockSpec((tm, S5_CH), lambda b, c: (b * nc + c, 0)),
        out_shape=jax.ShapeDtypeStruct((m, S5_CH), BF16),
        scratch_shapes=[
            pltpu.VMEM((tm, S5_LANES), F32), pltpu.VMEM((tm, S5_LANES), F32),
            pltpu.VMEM((S5_ROWS, S5_LANES), F32), pltpu.VMEM((S5_ROWS, S5_LANES), F32),
            pltpu.VMEM((tm, S5_CH), F32),
        ],
        compiler_params=_cparams(2),
        name="s5_mixer",
    )(u, wb(bb_r), wb(bb_i), tab(pr), tab(pi), wc(c_re), wc(c_im),
      d.reshape(1, S5_CH), w_glu.astype(BF16), b_glu.reshape(1, 2 * S5_CH))


def _m2_kernel(p_ref, sm_ref, smt_ref, cw_ref, cb_ref, arow_ref, brow_ref, acol_ref, bcol_ref,
               drow_ref, nw_ref, o_ref, s_ref, buf_ref, y_ref):
    t = M2_CHUNK
    n = M2_STATE
    hd = M2_HEADDIM

    @pl.when(pl.program_id(1) == 0)
    def _():
        s_ref[...] = jnp.zeros_like(s_ref)
        buf_ref[0:8, :] = jnp.zeros((8, M2_XBC), F32)

    z = p_ref[:, S5_CH:S5_CH + M2_INNER]
    xbc = _silu(_causal_conv(buf_ref, p_ref[:, S5_CH + M2_INNER:], cw_ref[...], t) + cb_ref[...])
    xs = xbc[:, 0:M2_INNER]
    bm = xbc[:, M2_INNER:M2_INNER + M2_GROUPS * n]
    cm = xbc[:, M2_INNER + M2_GROUPS * n:]

    dt_c = _softplus(sm_ref[...] + brow_ref[...])
    adt_c = dt_c * -jnp.exp(arow_ref[...])
    dt_r = _softplus(smt_ref[2 * DN_HEADS:, :] + bcol_ref[...])
    adt_r = dt_r * -jnp.exp(acol_ref[...])

    row = lax.broadcasted_iota(jnp.int32, (t, t), 0)
    col = lax.broadcasted_iota(jnp.int32, (t, t), 1)
    lower = row >= col
    tri = jnp.where(lower, 1.0, 0.0).astype(BF16)
    tri_t = jnp.where(row <= col, 1.0, 0.0).astype(BF16)
    acs_c = _dot_exact_lhs(tri, adt_c)
    acs_r = _dot_exact_rhs(adt_r, tri_t)

    for g in range(M2_GROUPS):
        bg = bm[:, g * n:(g + 1) * n].astype(BF16)
        cg = cm[:, g * n:(g + 1) * n].astype(BF16)
        cbm = lax.dot_general(cg, bg, (((1,), (1,)), ((), ())), preferred_element_type=F32)
        for j in range(M2_HPG):
            hh = g * M2_HPG + j
            ln = 2 * DN_HEADS + hh
            a_col = acs_c[:, ln:ln + 1]
            a_row = acs_r[hh:hh + 1, :]
            seg = jnp.exp(jnp.where(lower, a_col - a_row, -1e30))
            xh = xs[:, hh * hd:(hh + 1) * hd]
            xdt = xh * dt_c[:, ln:ln + 1]
            s_h = s_ref[hh]
            y = jnp.dot((cbm * seg).astype(BF16), xdt.astype(BF16), preferred_element_type=F32)
            y += lax.dot_general(cg, s_h.astype(BF16), (((1,), (1,)), ((), ())),
                                 preferred_element_type=F32) * jnp.exp(a_col)
            y += drow_ref[:, ln:ln + 1] * xh
            y_ref[:, hh * hd:(hh + 1) * hd] = y
            a_last = a_col[t - 1:t, :]
            xdec = xdt * jnp.exp(a_last - a_col)
            s_ref[hh] = s_h * jnp.exp(a_last) + lax.dot_general(
                xdec.astype(BF16), bg, (((0,), (0,)), ((), ())), preferred_element_type=F32)

    gw = M2_INNER // M2_GROUPS
    for g in range(M2_GROUPS):
        sl = slice(g * gw, (g + 1) * gw)
        y = y_ref[:, sl] * _silu(z[:, sl])
        y = y * lax.rsqrt(jnp.mean(y * y, axis=-1, keepdims=True) + NORM_EPS) * nw_ref[:, sl]
        o_ref[:, sl] = y.astype(o_ref.dtype)


def _mamba2(proj_m2, small, small_t, conv_w, conv_b, a_log, dt_bias, d, norm_w, bsz, seq):
    m = bsz * seq
    t = M2_CHUNK
    nc = seq // t
    lo = 2 * DN_HEADS
    lane_pad = lambda v: jnp.zeros((1, SMALL_W), F32).at[0, lo:lo + M2_HEADS].set(v.astype(F32))
    full = lambda *shape: pl.BlockSpec(shape, lambda b, c: (0,) * len(shape))
    return pl.pallas_call(
        _m2_kernel,
        grid=(bsz, nc),
        in_specs=[
            pl.BlockSpec((t, S5_CH + M2_INNER + M2_XBC), lambda b, c: (b * nc + c, 0)),
            pl.BlockSpec((t, SMALL_W), lambda b, c: (b * nc + c, 0)),
            pl.BlockSpec((None, SMALL_ROWS, t), lambda b, c: (b, 0, c)),
            full(CONV_K, M2_XBC), full(1, M2_XBC),
            full(1, SMALL_W), full(1, SMALL_W), full(M2_HEADS, 1), full(M2_HEADS, 1),
            full(1, SMALL_W), full(1, M2_INNER),
        ],
        out_specs=pl.BlockSpec((t, M2_INNER), lambda b, c: (b * nc + c, 0)),
        out_shape=jax.ShapeDtypeStruct((m, M2_INNER), BF16),
        scratch_shapes=[
            pltpu.VMEM((M2_HEADS, M2_HEADDIM, M2_STATE), F32),
            pltpu.VMEM((t + 8, M2_XBC), F32),
            pltpu.VMEM((t, M2_INNER), F32),
        ],
        compiler_params=_cparams(2),
        name="mamba2_ssd",
    )(proj_m2, small, small_t, conv_w, conv_b.reshape(1, M2_XBC),
      lane_pad(a_log), lane_pad(dt_bias), a_log.astype(F32).reshape(M2_HEADS, 1),
      dt_bias.astype(F32).reshape(M2_HEADS, 1), lane_pad(d), norm_w.reshape(1, M2_INNER))


def kernel(x, c, w_mod, b_mod, mod_table, norm_ffn1, norm_mix, norm_ffn2, ffn1_w_gate, ffn1_w_up, ffn1_w_down, ffn2_w_gate, ffn2_w_up, ffn2_w_down, w_in, w_out, dn_conv_w, dn_a_log, dn_dt_bias, dn_norm_w, s5_a_re, s5_a_im, s5_b_re, s5_b_im, s5_c_re, s5_c_im, s5_log_dt, s5_d, s5_w_glu, s5_b_glu, m2_conv_w, m2_conv_b, m2_a_log, m2_dt_bias, m2_d, m2_norm_w, final_norm):
    bsz, seq, _ = x.shape
    m = bsz * seq
    mod = _modulation(c, w_mod, b_mod, mod_table)
    xf = x.reshape(m, D_MODEL).astype(F32)

    def ffn(xf, modl, norm_w, wg, wu, wd, k0):
        h = _norm_modulate(xf, norm_w, modl, k0, k0 + 1, seq)
        a = _ffn_gateup(h, wg, wu)
        return _ffn_down(a, wd.astype(BF16), xf, modl, k0 + 2, seq)

    o = np.cumsum((0,) + IN_SIZES)
    n_dn = int(o[2])
    n_sm = int(o[7] - o[4])

    for i in range(DEPTH):
        modl = mod[i]
        xf = ffn(xf, modl, norm_ffn1[i], ffn1_w_gate[i], ffn1_w_up[i], ffn1_w_down[i], 0)

        h = _norm_modulate(xf, norm_mix[i], modl, 3, 4, seq)
        proj_dn = _matmul(h, w_in[i], 512, 0, n_dn, "in_proj_dn")
        proj_sm = _matmul(h, w_in[i][:, int(o[4]):int(o[7])], 512, 0, n_sm, "in_proj_s5_m2")
        small = _in_proj_small(h, w_in[i])
        small_t = jnp.transpose(small.reshape(bsz, seq, SMALL_W)[:, :, :SMALL_ROWS], (0, 2, 1))

        y_dn = _gated_deltanet(proj_dn, small, small_t, dn_conv_w[i], dn_a_log[i].astype(F32),
                               dn_dt_bias[i].astype(F32), dn_norm_w[i], bsz, seq)
        prep = _s5_prep(s5_a_re[i], s5_a_im[i], s5_log_dt[i], s5_b_re[i], s5_b_im[i])
        y_s5 = _s5_mixer(proj_sm, prep, s5_c_re[i], s5_c_im[i], s5_d[i], s5_w_glu[i], s5_b_glu[i],
                         bsz, seq)
        y_m2 = _mamba2(proj_sm, small, small_t, m2_conv_w[i], m2_conv_b[i], m2_a_log[i],
                       m2_dt_bias[i], m2_d[i], m2_norm_w[i], bsz, seq)
        xf = _out_proj(y_dn, y_s5, y_m2, w_out[i], xf, modl, 5, seq)

        xf = ffn(xf, modl, norm_ffn2[i], ffn2_w_gate[i], ffn2_w_up[i], ffn2_w_down[i], 6)

    return _final_norm(xf, final_norm).reshape(bsz, seq, D_MODEL)
```

```python
import functools

import jax
import jax.numpy as jnp
import numpy as np
from jax import lax
from jax.experimental import pallas as pl
from jax.experimental.pallas import tpu as pltpu

F32 = jnp.float32
BF16 = jnp.bfloat16

D_MODEL = 4096
DEPTH = 2
D_FF = 11008
FFN_RES = 0.5
N_MOD = 9
NORM_EPS = 1e-6
CONV_K = 4

DN_HEADS = 12
DN_DK = 128
DN_DV = 128
DN_BLOCK = 256
DN_BASE = 8
DN_HPS = 4

S5_CH = 1024
S5_GROUP = 16
S5_GROUPS = 64
S5_STATE = 64
S5_LANES = S5_GROUPS * S5_STATE
S5_TILE_G = 8
S5_ROWS = 8
S5_BLOCK = 256

M2_HEADS = 24
M2_HEADDIM = 64
M2_INNER = M2_HEADS * M2_HEADDIM
M2_GROUPS = 4
M2_HPG = M2_HEADS // M2_GROUPS
M2_STATE = 128
M2_CHUNK = 128

DN_WIDTH = DN_HEADS * DN_DV
DN_QKV = 2 * DN_HEADS * DN_DK + DN_HEADS * DN_DV
M2_XBC = M2_INNER + 2 * M2_GROUPS * M2_STATE
IN_SIZES = (DN_QKV, DN_WIDTH, DN_HEADS, DN_HEADS, S5_CH, M2_INNER, M2_XBC, M2_HEADS)
SMALL_W = 128
SMALL_ROWS = 48

LANE = 128
MM_ROWS = 1024
VMEM_LIMIT = 56 * 1024 * 1024


def _cparams(n_axes):
    return pltpu.CompilerParams(
        dimension_semantics=("arbitrary",) * n_axes, vmem_limit_bytes=VMEM_LIMIT)


def _sigmoid(x):
    return 1.0 / (1.0 + jnp.exp(-x))


def _silu(x):
    return x * _sigmoid(x)


def _softplus(x):
    return jnp.maximum(x, 0.0) + jnp.log(1.0 + jnp.exp(-jnp.abs(x)))


def _dot(a, b):
    return jnp.dot(a.astype(BF16), b.astype(BF16), preferred_element_type=F32)


def _dot_nt(a, b):
    return lax.dot_general(a.astype(BF16), b.astype(BF16), (((1,), (1,)), ((), ())),
                           preferred_element_type=F32)


def _dot_tn(a, b):
    return lax.dot_general(a.astype(BF16), b.astype(BF16), (((0,), (0,)), ((), ())),
                           preferred_element_type=F32)


def _split3(x):
    x1 = x.astype(BF16)
    r = x - x1.astype(F32)
    x2 = r.astype(BF16)
    x3 = (r - x2.astype(F32)).astype(BF16)
    return x1, x2, x3


def _dot_exact_lhs(mask_bf16, x):
    x1, x2, x3 = _split3(x)
    d = lambda t: jnp.dot(mask_bf16, t, preferred_element_type=F32)
    return d(x1) + d(x2) + d(x3)


def _dot_exact_rhs(x, mask_bf16):
    x1, x2, x3 = _split3(x)
    d = lambda t: jnp.dot(t, mask_bf16, preferred_element_type=F32)
    return d(x1) + d(x2) + d(x3)


def _dot_hi(a, b):
    a1 = a.astype(BF16)
    a2 = (a - a1.astype(F32)).astype(BF16)
    b1 = b.astype(BF16)
    b2 = (b - b1.astype(F32)).astype(BF16)
    d = lambda s, t: jnp.dot(s, t, preferred_element_type=F32)
    return d(a1, b1) + (d(a1, b2) + d(a2, b1))


def _mod_kernel(ct_ref, w_ref, b_ref, t_ref, o_ref, scb_ref):
    bsz = scb_ref.shape[0]

    @pl.when(pl.program_id(0) == 0)
    def _():
        sc = _silu(ct_ref[...])
        for b in range(bsz):
            scb_ref[b] = jnp.broadcast_to(sc[:, b:b + 1], (D_MODEL, LANE))

    for blk in range(w_ref.shape[1] // LANE):
        cs = slice(blk * LANE, (blk + 1) * LANE)
        w = w_ref[:, cs]
        for b in range(bsz):
            acc = jnp.sum(w * scb_ref[b], axis=0, keepdims=True) + b_ref[:, cs]
            for l in range(DEPTH):
                o_ref[l, b:b + 1, cs] = acc + t_ref[l, :, cs]


def _modulation(c, w_mod, b_mod, mod_table):
    bsz = c.shape[0]
    n = N_MOD * D_MODEL
    tn = 512
    out = pl.pallas_call(
        _mod_kernel,
        grid=(n // tn,),
        in_specs=[
            pl.BlockSpec((D_MODEL, bsz), lambda j: (0, 0)),
            pl.BlockSpec((D_MODEL, tn), lambda j: (0, j)),
            pl.BlockSpec((1, tn), lambda j: (0, j)),
            pl.BlockSpec((DEPTH, 1, tn), lambda j: (0, 0, j)),
        ],
        out_specs=pl.BlockSpec((DEPTH, bsz, tn), lambda j: (0, 0, j)),
        out_shape=jax.ShapeDtypeStruct((DEPTH, bsz, n), F32),
        scratch_shapes=[pltpu.VMEM((bsz, D_MODEL, LANE), F32)],
        compiler_params=_cparams(1),
        name="adaln_mod",
    )(c.astype(F32).T, w_mod, b_mod.reshape(1, n), mod_table.reshape(DEPTH, 1, n))
    return out.reshape(DEPTH, bsz * N_MOD, 1, D_MODEL)


def _normmod_kernel(x_ref, w_ref, sh_ref, sc_ref, o_ref):
    x = x_ref[...]
    y = x * lax.rsqrt(jnp.mean(x * x, axis=-1, keepdims=True) + NORM_EPS) * w_ref[...]
    o_ref[...] = (y * (1.0 + sc_ref[...]) + sh_ref[...]).astype(o_ref.dtype)


def _norm_kernel(x_ref, w_ref, o_ref):
    x = x_ref[...]
    y = x * lax.rsqrt(jnp.mean(x * x, axis=-1, keepdims=True) + NORM_EPS) * w_ref[...]
    o_ref[...] = y.astype(o_ref.dtype)


def _norm_modulate(x, w, modl, k_shift, k_scale, seq):
    m = x.shape[0]
    tm = 256
    per_b = seq // tm
    vec = lambda k: pl.BlockSpec((None, 1, D_MODEL), lambda i: ((i // per_b) * N_MOD + k, 0, 0))
    return pl.pallas_call(
        _normmod_kernel,
        grid=(m // tm,),
        in_specs=[
            pl.BlockSpec((tm, D_MODEL), lambda i: (i, 0)),
            pl.BlockSpec((1, D_MODEL), lambda i: (0, 0)),
            vec(k_shift),
            vec(k_scale),
        ],
        out_specs=pl.BlockSpec((tm, D_MODEL), lambda i: (i, 0)),
        out_shape=jax.ShapeDtypeStruct((m, D_MODEL), BF16),
        compiler_params=_cparams(1),
        name="norm_modulate",
    )(x, w.reshape(1, D_MODEL), modl, modl)


def _final_norm(x, w):
    m = x.shape[0]
    tm = 256
    return pl.pallas_call(
        _norm_kernel,
        grid=(m // tm,),
        in_specs=[
            pl.BlockSpec((tm, D_MODEL), lambda i: (i, 0)),
            pl.BlockSpec((1, D_MODEL), lambda i: (0, 0)),
        ],
        out_specs=pl.BlockSpec((tm, D_MODEL), lambda i: (i, 0)),
        out_shape=jax.ShapeDtypeStruct((m, D_MODEL), F32),
        compiler_params=_cparams(1),
        name="final_norm",
    )(x, w.reshape(1, D_MODEL))


def _first_row_tile():
    return pl.program_id(1) == 0


def _mm_kernel(a_ref, w_ref, o_ref, wb_ref):
    @pl.when(_first_row_tile())
    def _():
        wb_ref[...] = w_ref[...].astype(BF16)

    o_ref[...] = jnp.dot(a_ref[...], wb_ref[...], preferred_element_type=F32).astype(o_ref.dtype)


def _matmul(a, w, layer, tn, col0, n, name):
    m, k = a.shape
    tm = min(m, MM_ROWS)
    j0 = col0 // tn
    return pl.pallas_call(
        _mm_kernel,
        grid=(n // tn, m // tm),
        in_specs=[
            pl.BlockSpec((tm, k), lambda j, i: (i, 0)),
            pl.BlockSpec((None, k, tn), lambda j, i: (layer, 0, j0 + j)),
        ],
        out_specs=pl.BlockSpec((tm, tn), lambda j, i: (i, j)),
        out_shape=jax.ShapeDtypeStruct((m, n), F32),
        scratch_shapes=[pltpu.VMEM((k, tn), BF16)],
        compiler_params=_cparams(2),
        name=name,
    )(a, w)


def _mm_shift_kernel(a_ref, wa_ref, wb_ref, o_ref, wbuf_ref, *, shift):
    @pl.when(_first_row_tile())
    def _():
        w = jnp.concatenate([wa_ref[:, shift:], wb_ref[:, :shift]], axis=1)
        wbuf_ref[...] = w.astype(BF16)

    o_ref[...] = jnp.dot(a_ref[...], wbuf_ref[...], preferred_element_type=F32)


def _matmul_unaligned(a, w, layer, tn, col0, n, name):
    m, k = a.shape
    tm = min(m, MM_ROWS)
    j0, shift = divmod(col0, tn)
    return pl.pallas_call(
        functools.partial(_mm_shift_kernel, shift=shift),
        grid=(n // tn, m // tm),
        in_specs=[
            pl.BlockSpec((tm, k), lambda j, i: (i, 0)),
            pl.BlockSpec((None, k, tn), lambda j, i: (layer, 0, j0 + j)),
            pl.BlockSpec((None, k, tn), lambda j, i: (layer, 0, j0 + j + 1)),
        ],
        out_specs=pl.BlockSpec((tm, tn), lambda j, i: (i, j)),
        out_shape=jax.ShapeDtypeStruct((m, n), F32),
        scratch_shapes=[pltpu.VMEM((k, tn), BF16)],
        compiler_params=_cparams(2),
        name=name,
    )(a, w, w)


def _small_kernel(a_ref, wa_ref, wb_ref, o_ref):
    a = a_ref[...]
    lane = lax.broadcasted_iota(jnp.int32, (1, SMALL_W), 1)
    ya = _dot(a, wa_ref[...])
    yb = _dot(a, wb_ref[...])
    y = jnp.where(lane < 2 * DN_HEADS, ya, yb)
    o_ref[...] = jnp.where(lane < SMALL_ROWS, y, 0.0)


def _in_proj_small(a, w, layer):
    m, k = a.shape
    o = np.cumsum((0,) + IN_SIZES)
    assert o[2] % LANE == 0 and o[7] % LANE == 2 * DN_HEADS
    tm = min(m, MM_ROWS)
    return pl.pallas_call(
        _small_kernel,
        grid=(m // tm,),
        in_specs=[
            pl.BlockSpec((tm, k), lambda i: (i, 0)),
            pl.BlockSpec((None, k, LANE), lambda i: (layer, 0, int(o[2]) // LANE)),
            pl.BlockSpec((None, k, LANE), lambda i: (layer, 0, int(o[7]) // LANE)),
        ],
        out_specs=pl.BlockSpec((tm, SMALL_W), lambda i: (i, 0)),
        out_shape=jax.ShapeDtypeStruct((m, SMALL_W), F32),
        compiler_params=_cparams(1),
        name="in_proj_small",
    )(a, w, w)


def _gateup_kernel(h_ref, wg_ref, wu_ref, wd_ref, o_ref, wdb_ref, wgb_ref, wub_ref):
    @pl.when(_first_row_tile())
    def _():
        wgb_ref[...] = wg_ref[...].astype(BF16)
        wub_ref[...] = wu_ref[...].astype(BF16)

    h = h_ref[...]
    g = jnp.dot(h, wgb_ref[...], preferred_element_type=F32)
    u = jnp.dot(h, wub_ref[...], preferred_element_type=F32)
    o_ref[...] = (_silu(g) * u).astype(o_ref.dtype)
    wdb_ref[...] = wd_ref[...].astype(BF16)


def _ffn_gateup(h, wg, wu, wd, layer):
    m = h.shape[0]
    tm = min(m, MM_ROWS)
    tn = 256
    nj, ni = D_FF // tn, m // tm
    rows = D_FF // (nj * ni)
    wspec = pl.BlockSpec((None, D_MODEL, tn), lambda j, i: (layer, 0, j))
    return pl.pallas_call(
        _gateup_kernel,
        grid=(nj, ni),
        in_specs=[
            pl.BlockSpec((tm, D_MODEL), lambda j, i: (i, 0)),
            wspec, wspec,
            pl.BlockSpec((None, rows, D_MODEL), lambda j, i: (layer, j * ni + i, 0)),
        ],
        out_specs=(pl.BlockSpec((tm, tn), lambda j, i: (i, j)),
                   pl.BlockSpec((rows, D_MODEL), lambda j, i: (j * ni + i, 0))),
        out_shape=(jax.ShapeDtypeStruct((m, D_FF), BF16),
                   jax.ShapeDtypeStruct((D_FF, D_MODEL), BF16)),
        scratch_shapes=[pltpu.VMEM((D_MODEL, tn), BF16), pltpu.VMEM((D_MODEL, tn), BF16)],
        compiler_params=_cparams(2),
        name="ffn_gateup",
    )(h, wg, wu, wd)


def _down_kernel(a_ref, w_ref, x_ref, g_ref, o_ref, acc_ref):
    k = pl.program_id(2)

    @pl.when(k == 0)
    def _():
        acc_ref[...] = jnp.zeros_like(acc_ref)

    acc_ref[...] += jnp.dot(a_ref[...], w_ref[...], preferred_element_type=F32)

    @pl.when(k == pl.num_programs(2) - 1)
    def _():
        o_ref[...] = x_ref[...] + FFN_RES * g_ref[...] * acc_ref[...]


def _ffn_down(a, wd, x, modl, k_gate, seq):
    m = a.shape[0]
    tm = min(m, MM_ROWS)
    tn = 512
    tk = D_FF // 2
    per_b = seq // tm
    return pl.pallas_call(
        _down_kernel,
        grid=(m // tm, D_MODEL // tn, D_FF // tk),
        in_specs=[
            pl.BlockSpec((tm, tk), lambda i, j, k: (i, k)),
            pl.BlockSpec((tk, tn), lambda i, j, k: (k, j)),
            pl.BlockSpec((tm, tn), lambda i, j, k: (i, j)),
            pl.BlockSpec((None, 1, tn), lambda i, j, k: ((i // per_b) * N_MOD + k_gate, 0, j)),
        ],
        out_specs=pl.BlockSpec((tm, tn), lambda i, j, k: (i, j)),
        out_shape=jax.ShapeDtypeStruct((m, D_MODEL), F32),
        scratch_shapes=[pltpu.VMEM((tm, tn), F32)],
        compiler_params=_cparams(3),
        name="ffn_down",
    )(a, wd, x, modl)


def _outproj_kernel(ydn_ref, ys5_ref, ym2_ref, w_ref, x_ref, g_ref, o_ref, wb_ref):
    @pl.when(_first_row_tile())
    def _():
        wb_ref[...] = w_ref[...].astype(BF16)

    acc = jnp.dot(ydn_ref[...], wb_ref[0:DN_WIDTH, :], preferred_element_type=F32)
    acc += jnp.dot(ys5_ref[...], wb_ref[DN_WIDTH:DN_WIDTH + S5_CH, :], preferred_element_type=F32)
    acc += jnp.dot(ym2_ref[...], wb_ref[DN_WIDTH + S5_CH:, :], preferred_element_type=F32)
    o_ref[...] = x_ref[...] + g_ref[...] * acc


def _out_proj(y_dn, y_s5, y_m2, w, layer, x, modl, k_gate, seq):
    m = x.shape[0]
    tm = min(m, MM_ROWS)
    tn = 512
    per_b = seq // tm
    return pl.pallas_call(
        _outproj_kernel,
        grid=(D_MODEL // tn, m // tm),
        in_specs=[
            pl.BlockSpec((tm, DN_WIDTH), lambda j, i: (i, 0)),
            pl.BlockSpec((tm, S5_CH), lambda j, i: (i, 0)),
            pl.BlockSpec((tm, M2_INNER), lambda j, i: (i, 0)),
            pl.BlockSpec((None, D_MODEL, tn), lambda j, i: (layer, 0, j)),
            pl.BlockSpec((tm, tn), lambda j, i: (i, j)),
            pl.BlockSpec((None, 1, tn), lambda j, i: ((i // per_b) * N_MOD + k_gate, 0, j)),
        ],
        out_specs=pl.BlockSpec((tm, tn), lambda j, i: (i, j)),
        out_shape=jax.ShapeDtypeStruct((m, D_MODEL), F32),
        scratch_shapes=[pltpu.VMEM((D_MODEL, tn), BF16)],
        compiler_params=_cparams(2),
        name="out_proj",
    )(y_dn, y_s5, y_m2, w, x, modl)


def _causal_conv(buf_ref, raw, w, t):
    buf_ref[8:t + 8, :] = raw
    acc = raw * w[CONV_K - 1:CONV_K, :]
    for j in range(CONV_K - 1):
        off = 8 - (CONV_K - 1) + j
        acc += buf_ref[off:off + t, :] * w[j:j + 1, :]
    buf_ref[0:8, :] = raw[t - 8:t, :]
    return acc


def _dn_consts():
    idx = np.arange(DN_BLOCK)
    row, col = idx[:, None], idx[None, :]
    blk = lambda s: (row // s) == (col // s)
    strict = row > col
    masks = [blk(DN_BASE) & strict]
    s = DN_BASE
    while s < DN_BLOCK:
        masks.append(blk(2 * s) & ~blk(s) & strict)
        s *= 2
    tri = row >= col
    return (jnp.asarray(tri, BF16), jnp.asarray(tri.T, BF16),
            jnp.asarray(np.stack(masks), F32))


def _dn_kernel(q_ref, k_ref, v_ref, z_ref, sm_ref, smt_ref, wq_ref, wk_ref, wv_ref,
               alog_ref, dtb_ref, nw_ref, tri_ref, trit_ref, mask_ref, o_ref, s_ref, buf_ref):
    tb = DN_BLOCK

    @pl.when(pl.program_id(2) == 0)
    def _():
        s_ref[...] = jnp.zeros_like(s_ref)
        buf_ref[:, 0:8, :] = jnp.zeros((3, 8, DN_HPS * LANE), F32)

    q_all = _silu(_causal_conv(buf_ref.at[0], q_ref[...], wq_ref[...], tb))
    k_all = _silu(_causal_conv(buf_ref.at[1], k_ref[...], wk_ref[...], tb))
    v_all = _silu(_causal_conv(buf_ref.at[2], v_ref[...], wv_ref[...], tb))

    lane = lax.broadcasted_iota(jnp.int32, (1, SMALL_W), 1)
    sm = sm_ref[...]
    pick = lambda idx: jnp.sum(jnp.where(lane == idx, sm, 0.0), axis=1, keepdims=True)
    row = lax.broadcasted_iota(jnp.int32, (tb, tb), 0)
    col = lax.broadcasted_iota(jnp.int32, (tb, tb), 1)
    lower = row >= col
    one = jnp.ones((1, 1), F32)
    n_levels = mask_ref.shape[0]

    def head(i):
        h = pl.program_id(1) * DN_HPS + i
        sl = slice(i * LANE, (i + 1) * LANE)
        q, k, v = q_all[:, sl], k_all[:, sl], v_all[:, sl]
        q = q * lax.rsqrt(jnp.sum(q * q, axis=-1, keepdims=True) + 1e-6) * (DN_DK ** -0.5)
        k = k * lax.rsqrt(jnp.sum(k * k, axis=-1, keepdims=True) + 1e-6)

        coef = -jnp.exp(one * alog_ref[h])
        dtb = one * dtb_ref[h]
        g_c = coef * _softplus(pick(h) + dtb)
        g_r = coef * _softplus(smt_ref[pl.ds(h, 1), :] + dtb)
        beta_c = _sigmoid(pick(DN_HEADS + h))

        gcol = _dot_exact_lhs(tri_ref[...], jnp.broadcast_to(g_c, (tb, LANE)))
        gc_r = _dot_exact_rhs(jnp.broadcast_to(g_r, (8, tb)), trit_ref[...])
        yield
        gc_cc = jnp.concatenate([gcol] * (tb // LANE), axis=1)
        gc_rr = jnp.broadcast_to(gc_r[0:1, :], (tb, tb))
        decay = jnp.exp(jnp.where(lower, gc_cc - gc_rr, -1e30))
        egc = jnp.exp(gcol)

        kb = k * beta_c
        vb = v * beta_c
        kd = _dot_nt(kb, k) * decay
        qk = _dot_nt(q, k) * decay
        yield

        d1 = kd * mask_ref[0]
        d2 = _dot(d1, d1)
        yield
        d4 = _dot(d2, d2)
        yield
        nq = d2 + d4 + _dot(d2, d4)
        yield
        nn = nq - d1 - _dot(d1, nq)
        yield
        for lvl in range(1, n_levels):
            cpart = kd * mask_ref[lvl]
            x = cpart + _dot(cpart, nn)
            yield
            nn = nn - (x + _dot(nn, x))
            yield

        rhs = jnp.concatenate([vb, kb * egc], axis=1)
        sol = rhs + _dot(nn, rhs)
        yield
        u = sol[:, 0:DN_DV]
        w = sol[:, DN_DV:]

        s = s_ref[i]
        sb = s.astype(BF16)
        v_new = u - _dot(w, sb)
        o = _dot(q * egc, sb)
        yield
        o += _dot(qk, v_new)
        g_last = gcol[tb - 1:tb, :]
        s_ref[i] = s * jnp.exp(g_last[:, 0:1]) + _dot_tn(k * jnp.exp(g_last - gcol), v_new)
        yield

        o = o * lax.rsqrt(jnp.mean(o * o, axis=-1, keepdims=True) + NORM_EPS) * nw_ref[...]
        o_ref[:, sl] = (o * _silu(z_ref[:, sl])).astype(o_ref.dtype)

    done = object()
    chains = [head(i) for i in range(DN_HPS)]
    while chains:
        chains = [ch for ch in chains if next(ch, done) is not done]


def _gated_deltanet(proj_dn, small, small_t, conv_w, a_log, dt_bias, norm_w, bsz, seq):
    m = bsz * seq
    tb = DN_BLOCK
    nc = seq // tb
    hg = DN_HEADS // DN_HPS
    wd = DN_HPS * LANE
    tri, tri_t, masks = _dn_consts()
    tok = lambda off: pl.BlockSpec((tb, wd), lambda b, h, c: (b * nc + c, off * hg + h))
    cw = lambda off: pl.BlockSpec((CONV_K, wd), lambda b, h, c: (0, off * hg + h))
    smem = pl.BlockSpec(memory_space=pltpu.SMEM)
    full = lambda *shape: pl.BlockSpec(shape, lambda b, h, c: (0,) * len(shape))
    return pl.pallas_call(
        _dn_kernel,
        grid=(bsz, hg, nc),
        in_specs=[
            tok(0), tok(1), tok(2), tok(3),
            pl.BlockSpec((tb, SMALL_W), lambda b, h, c: (b * nc + c, 0)),
            pl.BlockSpec((None, SMALL_ROWS, tb), lambda b, h, c: (b, 0, c)),
            cw(0), cw(1), cw(2),
            smem, smem,
            full(1, DN_DV), full(tb, tb), full(tb, tb), full(*masks.shape),
        ],
        out_specs=pl.BlockSpec((tb, wd), lambda b, h, c: (b * nc + c, h)),
        out_shape=jax.ShapeDtypeStruct((m, DN_WIDTH), BF16),
        scratch_shapes=[
            pltpu.VMEM((DN_HPS, DN_DK, DN_DV), F32),
            pltpu.VMEM((3, tb + 8, wd), F32),
        ],
        compiler_params=_cparams(3),
        name="gated_deltanet",
    )(proj_dn, proj_dn, proj_dn, proj_dn, small, small_t, conv_w, conv_w, conv_w,
      a_log, dt_bias, norm_w.reshape(1, DN_DV), tri, tri_t, masks)


def _s5_prep_kernel(are_ref, aim_ref, ldt_ref, bre_ref, bim_ref, pr_ref, pi_ref, bbr_ref, bbi_ref):
    lr, li = are_ref[...], aim_ref[...]
    dt = jnp.exp(ldt_ref[...])
    mag = jnp.exp(lr * dt)
    ab_re, ab_im = mag * jnp.cos(li * dt), mag * jnp.sin(li * dt)
    den = lr * lr + li * li
    er, ei = ab_re - 1.0, ab_im
    fr = (lr * er + li * ei) / den
    fi = (lr * ei - li * er) / den
    for c in range(S5_GROUP):
        br, bi = bre_ref[c], bim_ref[c]
        bbr_ref[c] = fr * br - fi * bi
        bbi_ref[c] = fr * bi + fi * br
    pr, pi = ab_re, ab_im
    pr_ref[0], pi_ref[0] = pr, pi
    for n in range(1, S5_ROWS):
        pr, pi = pr * ab_re - pi * ab_im, pr * ab_im + pi * ab_re
        pr_ref[n], pi_ref[n] = pr, pi


def _s5_prep(a_re, a_im, log_dt, b_re, b_im):
    g, p = S5_GROUPS, S5_STATE
    shp = lambda n: jax.ShapeDtypeStruct((n, g, p), F32)
    return pl.pallas_call(
        _s5_prep_kernel,
        out_shape=(shp(S5_ROWS), shp(S5_ROWS), shp(S5_GROUP), shp(S5_GROUP)),
        name="s5_prep",
    )(a_re, a_im, log_dt.reshape(g, 1), jnp.transpose(b_re, (2, 0, 1)), jnp.transpose(b_im, (2, 0, 1)))


def _gelu_tanh(x):
    return 0.5 * x * (1.0 + jnp.tanh(np.sqrt(2.0 / np.pi).astype(np.float32) * (x + 0.044715 * (x * x * x))))


def _s5_kernel(u_ref, wbr_ref, wbi_ref, tr_ref, ti_ref, wcr_ref, wci_ref, d_ref, wglu_ref, bglu_ref,
               o_ref, xr_ref, xi_ref, cr_ref, ci_ref, y_ref):
    tm = S5_BLOCK
    kt = S5_TILE_G * S5_GROUP
    nt = S5_TILE_G * S5_STATE

    @pl.when(pl.program_id(1) == 0)
    def _():
        cr_ref[...] = jnp.zeros_like(cr_ref)
        ci_ref[...] = jnp.zeros_like(ci_ref)

    u = u_ref[...]
    ub = u.astype(BF16)
    for t in range(S5_GROUPS // S5_TILE_G):
        ut = ub[:, t * kt:(t + 1) * kt]
        xr_ref[:, t * nt:(t + 1) * nt] = jnp.dot(ut, wbr_ref[t], preferred_element_type=F32)
        xi_ref[:, t * nt:(t + 1) * nt] = jnp.dot(ut, wbi_ref[t], preferred_element_type=F32)

    def block(i, _):
        r0 = pl.multiple_of(i * S5_ROWS, S5_ROWS)
        for cb in range(S5_LANES // LANE):
            cs = slice(cb * LANE, (cb + 1) * LANE)
            xr = xr_ref[pl.ds(r0, S5_ROWS), cs]
            xi = xi_ref[pl.ds(r0, S5_ROWS), cs]
            for n, d in enumerate((1, 2, 4)):
                ar, ai = tr_ref[n, :, cs], ti_ref[n, :, cs]
                sr, si = pltpu.roll(xr, d, 0), pltpu.roll(xi, d, 0)
                xr, xi = xr + (ar * sr - ai * si), xi + (ar * si + ai * sr)
            pr, pi = tr_ref[3, :, cs], ti_ref[3, :, cs]
            cr, ci = cr_ref[:, cs], ci_ref[:, cs]
            xr, xi = xr + (pr * cr - pi * ci), xi + (pr * ci + pi * cr)
            xr_ref[pl.ds(r0, S5_ROWS), cs] = xr
            xi_ref[pl.ds(r0, S5_ROWS), cs] = xi
            cr_ref[:, cs] = jnp.broadcast_to(xr[S5_ROWS - 1:S5_ROWS, :], (S5_ROWS, LANE))
            ci_ref[:, cs] = jnp.broadcast_to(xi[S5_ROWS - 1:S5_ROWS, :], (S5_ROWS, LANE))
        return 0

    lax.fori_loop(0, tm // S5_ROWS, block, 0)

    for t in range(S5_GROUPS // S5_TILE_G):
        sr = xr_ref[:, t * nt:(t + 1) * nt].astype(BF16)
        si = xi_ref[:, t * nt:(t + 1) * nt].astype(BF16)
        yt = jnp.dot(sr, wcr_ref[t], preferred_element_type=F32)
        yt -= jnp.dot(si, wci_ref[t], preferred_element_type=F32)
        y_ref[:, t * kt:(t + 1) * kt] = yt + d_ref[:, t * kt:(t + 1) * kt] * u[:, t * kt:(t + 1) * kt]

    hg = jnp.dot(_gelu_tanh(y_ref[...]).astype(BF16), wglu_ref[...], preferred_element_type=F32)
    hg += bglu_ref[...]
    o_ref[...] = (hg[:, 0:S5_CH] * _sigmoid(hg[:, S5_CH:])).astype(o_ref.dtype)


def _s5_mixer(u, prep, c_re, c_im, d, w_glu, b_glu, bsz, seq):
    pw_r, pw_i, bb_r, bb_i = prep
    g, p, tg = S5_GROUPS, S5_STATE, S5_TILE_G
    nt = g // tg
    eye = jnp.eye(tg, dtype=F32)
    wb = lambda bb: jnp.einsum("ctip,ij->ticjp", bb.reshape(S5_GROUP, nt, tg, p), eye).reshape(
        nt, tg * S5_GROUP, tg * p).astype(BF16)
    wc = lambda cc: jnp.einsum("ticp,ij->tjpic", cc.reshape(nt, tg, S5_GROUP, p), eye).reshape(
        nt, tg * p, tg * S5_GROUP).astype(BF16)
    pr, pi = pw_r.reshape(S5_ROWS, S5_LANES), pw_i.reshape(S5_ROWS, S5_LANES)
    rows = jnp.arange(S5_ROWS)[:, None]
    tab = lambda pw: jnp.stack(
        [jnp.where(rows >= dd, pw[dd - 1][None, :], 0.0) for dd in (1, 2, 4)] + [pw])
    m = bsz * seq
    tm = S5_BLOCK
    nc = seq // tm
    full = lambda *shape: pl.BlockSpec(shape, lambda b, c: (0,) * len(shape))
    return pl.pallas_call(
        _s5_kernel,
        grid=(bsz, nc),
        in_specs=[
            pl.BlockSpec((tm, S5_CH), lambda b, c: (b * nc + c, 0)),
            full(nt, tg * S5_GROUP, tg * p), full(nt, tg * S5_GROUP, tg * p),
            full(4, S5_ROWS, S5_LANES), full(4, S5_ROWS, S5_LANES),
            full(nt, tg * p, tg * S5_GROUP), full(nt, tg * p, tg * S5_GROUP),
            full(1, S5_CH), full(S5_CH, 2 * S5_CH), full(1, 2 * S5_CH),
        ],
        out_specs=pl.BlockSpec((tm, S5_CH), lambda b, c: (b * nc + c, 0)),
        out_shape=jax.ShapeDtypeStruct((m, S5_CH), BF16),
        scratch_shapes=[
            pltpu.VMEM((tm, S5_LANES), F32), pltpu.VMEM((tm, S5_LANES), F32),
            pltpu.VMEM((S5_ROWS, S5_LANES), F32), pltpu.VMEM((S5_ROWS, S5_LANES), F32),
            pltpu.VMEM((tm, S5_CH), F32),
        ],
        compiler_params=_cparams(2),
        name="s5_mixer",
    )(u, wb(bb_r), wb(bb_i), tab(pr), tab(pi), wc(c_re), wc(c_im),
      d.reshape(1, S5_CH), w_glu.astype(BF16), b_glu.reshape(1, 2 * S5_CH))


def _m2_kernel(p_ref, sm_ref, smt_ref, cw_ref, cb_ref, arow_ref, brow_ref, acol_ref, bcol_ref,
               drow_ref, nw_ref, o_ref, s_ref, buf_ref, y_ref):
    t = M2_CHUNK
    n = M2_STATE
    hd = M2_HEADDIM

    @pl.when(pl.program_id(1) == 0)
    def _():
        s_ref[...] = jnp.zeros_like(s_ref)
        buf_ref[0:8, :] = jnp.zeros((8, M2_XBC), F32)

    z = p_ref[:, S5_CH:S5_CH + M2_INNER]
    xbc = _silu(_causal_conv(buf_ref, p_ref[:, S5_CH + M2_INNER:], cw_ref[...], t) + cb_ref[...])
    xs = xbc[:, 0:M2_INNER]
    bm = xbc[:, M2_INNER:M2_INNER + M2_GROUPS * n]
    cm = xbc[:, M2_INNER + M2_GROUPS * n:]

    dt_c = _softplus(sm_ref[...] + brow_ref[...])
    adt_c = dt_c * -jnp.exp(arow_ref[...])
    dt_r = _softplus(smt_ref[2 * DN_HEADS:, :] + bcol_ref[...])
    adt_r = dt_r * -jnp.exp(acol_ref[...])

    row = lax.broadcasted_iota(jnp.int32, (t, t), 0)
    col = lax.broadcasted_iota(jnp.int32, (t, t), 1)
    lower = row >= col
    tri = jnp.where(lower, 1.0, 0.0).astype(BF16)
    tri_t = jnp.where(row <= col, 1.0, 0.0).astype(BF16)
    acs_c = _dot_exact_lhs(tri, adt_c)
    acs_r = _dot_exact_rhs(adt_r, tri_t)

    for g in range(M2_GROUPS):
        bg = bm[:, g * n:(g + 1) * n].astype(BF16)
        cg = cm[:, g * n:(g + 1) * n].astype(BF16)
        cbm = lax.dot_general(cg, bg, (((1,), (1,)), ((), ())), preferred_element_type=F32)
        for j in range(M2_HPG):
            hh = g * M2_HPG + j
            ln = 2 * DN_HEADS + hh
            a_col = acs_c[:, ln:ln + 1]
            a_row = acs_r[hh:hh + 1, :]
            seg = jnp.exp(jnp.where(lower, a_col - a_row, -1e30))
            xh = xs[:, hh * hd:(hh + 1) * hd]
            xdt = xh * dt_c[:, ln:ln + 1]
            s_h = s_ref[hh]
            y = jnp.dot((cbm * seg).astype(BF16), xdt.astype(BF16), preferred_element_type=F32)
            y += lax.dot_general(cg, s_h.astype(BF16), (((1,), (1,)), ((), ())),
                                 preferred_element_type=F32) * jnp.exp(a_col)
            y += drow_ref[:, ln:ln + 1] * xh
            y_ref[:, hh * hd:(hh + 1) * hd] = y
            a_last = a_col[t - 1:t, :]
            xdec = xdt * jnp.exp(a_last - a_col)
            s_ref[hh] = s_h * jnp.exp(a_last) + lax.dot_general(
                xdec.astype(BF16), bg, (((0,), (0,)), ((), ())), preferred_element_type=F32)

    gw = M2_INNER // M2_GROUPS
    for g in range(M2_GROUPS):
        sl = slice(g * gw, (g + 1) * gw)
        y = y_ref[:, sl] * _silu(z[:, sl])
        y = y * lax.rsqrt(jnp.mean(y * y, axis=-1, keepdims=True) + NORM_EPS) * nw_ref[:, sl]
        o_ref[:, sl] = y.astype(o_ref.dtype)


def _mamba2(proj_m2, small, small_t, conv_w, conv_b, a_log, dt_bias, d, norm_w, bsz, seq):
    m = bsz * seq
    t = M2_CHUNK
    nc = seq // t
    lo = 2 * DN_HEADS
    lane_pad = lambda v: jnp.zeros((1, SMALL_W), F32).at[0, lo:lo + M2_HEADS].set(v.astype(F32))
    full = lambda *shape: pl.BlockSpec(shape, lambda b, c: (0,) * len(shape))
    return pl.pallas_call(
        _m2_kernel,
        grid=(bsz, nc),
        in_specs=[
            pl.BlockSpec((t, S5_CH + M2_INNER + M2_XBC), lambda b, c: (b * nc + c, 0)),
            pl.BlockSpec((t, SMALL_W), lambda b, c: (b * nc + c, 0)),
            pl.BlockSpec((None, SMALL_ROWS, t), lambda b, c: (b, 0, c)),
            full(CONV_K, M2_XBC), full(1, M2_XBC),
            full(1, SMALL_W), full(1, SMALL_W), full(M2_HEADS, 1), full(M2_HEADS, 1),
            full(1, SMALL_W), full(1, M2_INNER),
        ],
        out_specs=pl.BlockSpec((t, M2_INNER), lambda b, c: (b * nc + c, 0)),
        out_shape=jax.ShapeDtypeStruct((m, M2_INNER), BF16),
        scratch_shapes=[
            pltpu.VMEM((M2_HEADS, M2_HEADDIM, M2_STATE), F32),
            pltpu.VMEM((t + 8, M2_XBC), F32),
            pltpu.VMEM((t, M2_INNER), F32),
        ],
        compiler_params=_cparams(2),
        name="mamba2_ssd",
    )(proj_m2, small, small_t, conv_w, conv_b.reshape(1, M2_XBC),
      lane_pad(a_log), lane_pad(dt_bias), a_log.astype(F32).reshape(M2_HEADS, 1),
      dt_bias.astype(F32).reshape(M2_HEADS, 1), lane_pad(d), norm_w.reshape(1, M2_INNER))


def kernel(x, c, w_mod, b_mod, mod_table, norm_ffn1, norm_mix, norm_ffn2, ffn1_w_gate, ffn1_w_up, ffn1_w_down, ffn2_w_gate, ffn2_w_up, ffn2_w_down, w_in, w_out, dn_conv_w, dn_a_log, dn_dt_bias, dn_norm_w, s5_a_re, s5_a_im, s5_b_re, s5_b_im, s5_c_re, s5_c_im, s5_log_dt, s5_d, s5_w_glu, s5_b_glu, m2_conv_w, m2_conv_b, m2_a_log, m2_dt_bias, m2_d, m2_norm_w, final_norm):
    bsz, seq, _ = x.shape
    m = bsz * seq
    mod = _modulation(c, w_mod, b_mod, mod_table)
    xf = x.reshape(m, D_MODEL).astype(F32)

    def ffn(xf, modl, norm_w, wg, wu, wd, layer, k0):
        h = _norm_modulate(xf, norm_w, modl, k0, k0 + 1, seq)
        a, wd_bf16 = _ffn_gateup(h, wg, wu, wd, layer)
        return _ffn_down(a, wd_bf16, xf, modl, k0 + 2, seq)

    o = np.cumsum((0,) + IN_SIZES)
    n_dn = int(o[2])
    n_sm = int(o[7] - o[4])

    for i in range(DEPTH):
        modl = mod[i]
        xf = ffn(xf, modl, norm_ffn1[i], ffn1_w_gate, ffn1_w_up, ffn1_w_down, i, 0)

        h = _norm_modulate(xf, norm_mix[i], modl, 3, 4, seq)
        proj_dn = _matmul(h, w_in, i, 512, 0, n_dn, "in_proj_dn")
        proj_sm = _matmul_unaligned(h, w_in, i, 256, int(o[4]), n_sm, "in_proj_s5_m2")
        small = _in_proj_small(h, w_in, i)
        small_t = jnp.transpose(small.reshape(bsz, seq, SMALL_W)[:, :, :SMALL_ROWS], (0, 2, 1))

        y_dn = _gated_deltanet(proj_dn, small, small_t, dn_conv_w[i], dn_a_log[i].astype(F32),
                               dn_dt_bias[i].astype(F32), dn_norm_w[i], bsz, seq)
        prep = _s5_prep(s5_a_re[i], s5_a_im[i], s5_log_dt[i], s5_b_re[i], s5_b_im[i])
        y_s5 = _s5_mixer(proj_sm, prep, s5_c_re[i], s5_c_im[i], s5_d[i], s5_w_glu[i], s5_b_glu[i],
                         bsz, seq)
        y_m2 = _mamba2(proj_sm, small, small_t, m2_conv_w[i], m2_conv_b[i], m2_a_log[i],
                       m2_dt_bias[i], m2_d[i], m2_norm_w[i], bsz, seq)
        xf = _out_proj(y_dn, y_s5, y_m2, w_out, i, xf, modl, 5, seq)

        xf = ffn(xf, modl, norm_ffn2[i], ffn2_w_gate, ffn2_w_up, ffn2_w_down, i, 6)

    return _final_norm(xf, final_norm).reshape(bsz, seq, D_MODEL)
```

```python
import functools

import jax
import jax.numpy as jnp
import numpy as np
from jax import lax
from jax.experimental import pallas as pl
from jax.experimental.pallas import tpu as pltpu

F32 = jnp.float32
BF16 = jnp.bfloat16

D_MODEL = 4096
DEPTH = 2
D_FF = 11008
FFN_RES = 0.5
N_MOD = 9
NORM_EPS = 1e-6
CONV_K = 4

DN_HEADS = 12
DN_DK = 128
DN_DV = 128
DN_BLOCK = 256
DN_BASE = 8
DN_HPS = 4

S5_CH = 1024
S5_GROUP = 16
S5_GROUPS = 64
S5_STATE = 64
S5_LANES = S5_GROUPS * S5_STATE
S5_TILE_G = 8
S5_ROWS = 8
S5_BLOCK = 256

M2_HEADS = 24
M2_HEADDIM = 64
M2_INNER = M2_HEADS * M2_HEADDIM
M2_GROUPS = 4
M2_HPG = M2_HEADS // M2_GROUPS
M2_STATE = 128
M2_CHUNK = 128

DN_WIDTH = DN_HEADS * DN_DV
DN_QKV = 2 * DN_HEADS * DN_DK + DN_HEADS * DN_DV
M2_XBC = M2_INNER + 2 * M2_GROUPS * M2_STATE
IN_SIZES = (DN_QKV, DN_WIDTH, DN_HEADS, DN_HEADS, S5_CH, M2_INNER, M2_XBC, M2_HEADS)
SMALL_W = 128
SMALL_ROWS = 48

LANE = 128
MM_ROWS = 1024
MM_KCHUNK = 1024
VMEM_LIMIT = 56 * 1024 * 1024


def _cparams(n_axes):
    return pltpu.CompilerParams(
        dimension_semantics=("arbitrary",) * n_axes, vmem_limit_bytes=VMEM_LIMIT)


def _sigmoid(x):
    return 1.0 / (1.0 + jnp.exp(-x))


def _silu(x):
    return x * _sigmoid(x)


def _softplus(x):
    return jnp.maximum(x, 0.0) + jnp.log(1.0 + jnp.exp(-jnp.abs(x)))


def _dot(a, b):
    return jnp.dot(a.astype(BF16), b.astype(BF16), preferred_element_type=F32)


def _dot_nt(a, b):
    return lax.dot_general(a.astype(BF16), b.astype(BF16), (((1,), (1,)), ((), ())),
                           preferred_element_type=F32)


def _dot_tn(a, b):
    return lax.dot_general(a.astype(BF16), b.astype(BF16), (((0,), (0,)), ((), ())),
                           preferred_element_type=F32)


def _split3(x):
    x1 = x.astype(BF16)
    r = x - x1.astype(F32)
    x2 = r.astype(BF16)
    x3 = (r - x2.astype(F32)).astype(BF16)
    return x1, x2, x3


def _dot_exact_lhs(mask_bf16, x):
    x1, x2, x3 = _split3(x)
    d = lambda t: jnp.dot(mask_bf16, t, preferred_element_type=F32)
    return d(x1) + d(x2) + d(x3)


def _dot_exact_rhs(x, mask_bf16):
    x1, x2, x3 = _split3(x)
    d = lambda t: jnp.dot(t, mask_bf16, preferred_element_type=F32)
    return d(x1) + d(x2) + d(x3)


def _dot_hi(a, b):
    a1 = a.astype(BF16)
    a2 = (a - a1.astype(F32)).astype(BF16)
    b1 = b.astype(BF16)
    b2 = (b - b1.astype(F32)).astype(BF16)
    d = lambda s, t: jnp.dot(s, t, preferred_element_type=F32)
    return d(a1, b1) + (d(a1, b2) + d(a2, b1))


def _mod_kernel(ct_ref, w_ref, b_ref, t_ref, o_ref, scb_ref):
    bsz = scb_ref.shape[0]

    @pl.when(pl.program_id(0) == 0)
    def _():
        sc = _silu(ct_ref[...])
        for b in range(bsz):
            scb_ref[b] = jnp.broadcast_to(sc[:, b:b + 1], (D_MODEL, LANE))

    for blk in range(w_ref.shape[1] // LANE):
        cs = slice(blk * LANE, (blk + 1) * LANE)
        w = w_ref[:, cs]
        for b in range(bsz):
            acc = jnp.sum(w * scb_ref[b], axis=0, keepdims=True) + b_ref[:, cs]
            for l in range(DEPTH):
                o_ref[l, b:b + 1, cs] = acc + t_ref[l, :, cs]


def _modulation(c, w_mod, b_mod, mod_table):
    bsz = c.shape[0]
    n = N_MOD * D_MODEL
    tn = 512
    out = pl.pallas_call(
        _mod_kernel,
        grid=(n // tn,),
        in_specs=[
            pl.BlockSpec((D_MODEL, bsz), lambda j: (0, 0)),
            pl.BlockSpec((D_MODEL, tn), lambda j: (0, j)),
            pl.BlockSpec((1, tn), lambda j: (0, j)),
            pl.BlockSpec((DEPTH, 1, tn), lambda j: (0, 0, j)),
        ],
        out_specs=pl.BlockSpec((DEPTH, bsz, tn), lambda j: (0, 0, j)),
        out_shape=jax.ShapeDtypeStruct((DEPTH, bsz, n), F32),
        scratch_shapes=[pltpu.VMEM((bsz, D_MODEL, LANE), F32)],
        compiler_params=_cparams(1),
        name="adaln_mod",
    )(c.astype(F32).T, w_mod, b_mod.reshape(1, n), mod_table.reshape(DEPTH, 1, n))
    return out.reshape(DEPTH, bsz * N_MOD, 1, D_MODEL)


def _normmod_kernel(x_ref, w_ref, sh_ref, sc_ref, o_ref):
    x = x_ref[...]
    y = x * lax.rsqrt(jnp.mean(x * x, axis=-1, keepdims=True) + NORM_EPS) * w_ref[...]
    o_ref[...] = (y * (1.0 + sc_ref[...]) + sh_ref[...]).astype(o_ref.dtype)


def _norm_kernel(x_ref, w_ref, o_ref):
    x = x_ref[...]
    y = x * lax.rsqrt(jnp.mean(x * x, axis=-1, keepdims=True) + NORM_EPS) * w_ref[...]
    o_ref[...] = y.astype(o_ref.dtype)


def _norm_modulate(x, w, modl, k_shift, k_scale, seq):
    m = x.shape[0]
    tm = 256
    per_b = seq // tm
    vec = lambda k: pl.BlockSpec((None, 1, D_MODEL), lambda i: ((i // per_b) * N_MOD + k, 0, 0))
    return pl.pallas_call(
        _normmod_kernel,
        grid=(m // tm,),
        in_specs=[
            pl.BlockSpec((tm, D_MODEL), lambda i: (i, 0)),
            pl.BlockSpec((1, D_MODEL), lambda i: (0, 0)),
            vec(k_shift),
            vec(k_scale),
        ],
        out_specs=pl.BlockSpec((tm, D_MODEL), lambda i: (i, 0)),
        out_shape=jax.ShapeDtypeStruct((m, D_MODEL), BF16),
        compiler_params=_cparams(1),
        name="norm_modulate",
    )(x, w.reshape(1, D_MODEL), modl, modl)


def _final_norm(x, w):
    m = x.shape[0]
    tm = 256
    return pl.pallas_call(
        _norm_kernel,
        grid=(m // tm,),
        in_specs=[
            pl.BlockSpec((tm, D_MODEL), lambda i: (i, 0)),
            pl.BlockSpec((1, D_MODEL), lambda i: (0, 0)),
        ],
        out_specs=pl.BlockSpec((tm, D_MODEL), lambda i: (i, 0)),
        out_shape=jax.ShapeDtypeStruct((m, D_MODEL), F32),
        compiler_params=_cparams(1),
        name="final_norm",
    )(x, w.reshape(1, D_MODEL))


_NT = (((1,), (1,)), ((), ()))


def _dot_chunked(a_ref, w_ref, kc):
    acc = None
    for k0 in range(0, a_ref.shape[1], kc):
        p = jnp.dot(a_ref[:, k0:k0 + kc], w_ref[k0:k0 + kc, :].astype(BF16),
                    preferred_element_type=F32)
        acc = p if acc is None else acc + p
    return acc


def _dot_chunked_nt(a_ref, wt_ref, kc):
    acc = None
    for k0 in range(0, a_ref.shape[1], kc):
        p = lax.dot_general(a_ref[:, k0:k0 + kc], wt_ref[:, k0:k0 + kc].astype(BF16), _NT,
                            preferred_element_type=F32)
        acc = p if acc is None else acc + p
    return acc


def _rows_spec(rows, k, index_map):
    return pl.BlockSpec((pl.Element(1), pl.Element(rows), pl.Element(k)), index_map)


def _mm_nt_kernel(a_ref, wt_ref, o_ref):
    o_ref[...] = _dot_chunked_nt(a_ref, wt_ref.at[0], MM_KCHUNK)


def _matmul_nt(a, wt, layer, tn, row0, n, name):
    m, k = a.shape
    tm = min(m, MM_ROWS)
    return pl.pallas_call(
        _mm_nt_kernel,
        grid=(m // tm, n // tn),
        in_specs=[
            pl.BlockSpec((tm, k), lambda i, j: (i, 0)),
            _rows_spec(tn, k, lambda i, j: (layer, pl.multiple_of(row0 + j * tn, 8), 0)),
        ],
        out_specs=pl.BlockSpec((tm, tn), lambda i, j: (i, j)),
        out_shape=jax.ShapeDtypeStruct((m, n), F32),
        compiler_params=_cparams(2),
        name=name,
    )(a, wt)


def _small_kernel(a_ref, wa_ref, wb_ref, o_ref):
    k = a_ref.shape[1]
    pad = jnp.zeros((SMALL_W - SMALL_ROWS, k), F32)
    wt = jnp.concatenate([wa_ref[0], wb_ref[0], pad], axis=0).astype(BF16)
    o_ref[...] = lax.dot_general(a_ref[...], wt, _NT, preferred_element_type=F32)


def _in_proj_small(a, wt, layer):
    m, k = a.shape
    o = np.cumsum((0,) + IN_SIZES)
    half = SMALL_ROWS // 2
    assert int(o[4] - o[2]) == half and int(o[8] - o[7]) == half
    tm = min(m, MM_ROWS)
    grp = lambda r0: _rows_spec(half, k, lambda i: (layer, r0, 0))
    return pl.pallas_call(
        _small_kernel,
        grid=(m // tm,),
        in_specs=[pl.BlockSpec((tm, k), lambda i: (i, 0)), grp(int(o[2])), grp(int(o[7]))],
        out_specs=pl.BlockSpec((tm, SMALL_W), lambda i: (i, 0)),
        out_shape=jax.ShapeDtypeStruct((m, SMALL_W), F32),
        compiler_params=_cparams(1),
        name="in_proj_small",
    )(a, wt, wt)


def _gateup_kernel(h_ref, wg_ref, wu_ref, wd_ref, o_ref, wdb_ref):
    g = _dot_chunked(h_ref, wg_ref, MM_KCHUNK)
    u = _dot_chunked(h_ref, wu_ref, MM_KCHUNK)
    o_ref[...] = (_silu(g) * u).astype(o_ref.dtype)
    wdb_ref[...] = wd_ref[...].astype(BF16)


def _ffn_gateup(h, wg, wu, wd, layer):
    m = h.shape[0]
    tm = min(m, MM_ROWS)
    tn = 256
    ni, nj = m // tm, D_FF // tn
    rows = D_FF // (nj * ni)
    wspec = pl.BlockSpec((None, D_MODEL, tn), lambda i, j: (layer, 0, j))
    return pl.pallas_call(
        _gateup_kernel,
        grid=(ni, nj),
        in_specs=[
            pl.BlockSpec((tm, D_MODEL), lambda i, j: (i, 0)),
            wspec, wspec,
            pl.BlockSpec((None, rows, D_MODEL), lambda i, j: (layer, i * nj + j, 0)),
        ],
        out_specs=(pl.BlockSpec((tm, tn), lambda i, j: (i, j)),
                   pl.BlockSpec((rows, D_MODEL), lambda i, j: (i * nj + j, 0))),
        out_shape=(jax.ShapeDtypeStruct((m, D_FF), BF16),
                   jax.ShapeDtypeStruct((D_FF, D_MODEL), BF16)),
        compiler_params=_cparams(2),
        name="ffn_gateup",
    )(h, wg, wu, wd)


def _down_kernel(a_ref, w_ref, x_ref, g_ref, o_ref, acc_ref):
    k = pl.program_id(2)

    @pl.when(k == 0)
    def _():
        acc_ref[...] = jnp.zeros_like(acc_ref)

    acc_ref[...] += jnp.dot(a_ref[...], w_ref[...], preferred_element_type=F32)

    @pl.when(k == pl.num_programs(2) - 1)
    def _():
        o_ref[...] = x_ref[...] + FFN_RES * g_ref[...] * acc_ref[...]


def _ffn_down(a, wd, x, modl, k_gate, seq):
    m = a.shape[0]
    tm = min(m, MM_ROWS)
    tn = 512
    tk = D_FF // 2
    per_b = seq // tm
    return pl.pallas_call(
        _down_kernel,
        grid=(m // tm, D_MODEL // tn, D_FF // tk),
        in_specs=[
            pl.BlockSpec((tm, tk), lambda i, j, k: (i, k)),
            pl.BlockSpec((tk, tn), lambda i, j, k: (k, j)),
            pl.BlockSpec((tm, tn), lambda i, j, k: (i, j)),
            pl.BlockSpec((None, 1, tn), lambda i, j, k: ((i // per_b) * N_MOD + k_gate, 0, j)),
        ],
        out_specs=pl.BlockSpec((tm, tn), lambda i, j, k: (i, j)),
        out_shape=jax.ShapeDtypeStruct((m, D_MODEL), F32),
        scratch_shapes=[pltpu.VMEM((tm, tn), F32)],
        compiler_params=_cparams(3),
        name="ffn_down",
    )(a, wd, x, modl)


def _outproj_kernel(ydn_ref, ys5_ref, ym2_ref, w_ref, x_ref, g_ref, o_ref):
    lo = 0
    acc = None
    for y_ref in (ydn_ref, ys5_ref, ym2_ref):
        hi = lo + y_ref.shape[1]
        p = jnp.dot(y_ref[...], w_ref[lo:hi, :].astype(BF16), preferred_element_type=F32)
        acc = p if acc is None else acc + p
        lo = hi
    o_ref[...] = x_ref[...] + g_ref[...] * acc


def _out_proj(y_dn, y_s5, y_m2, w, layer, x, modl, k_gate, seq):
    m = x.shape[0]
    tm = min(m, MM_ROWS)
    tn = 512
    per_b = seq // tm
    return pl.pallas_call(
        _outproj_kernel,
        grid=(m // tm, D_MODEL // tn),
        in_specs=[
            pl.BlockSpec((tm, DN_WIDTH), lambda i, j: (i, 0)),
            pl.BlockSpec((tm, S5_CH), lambda i, j: (i, 0)),
            pl.BlockSpec((tm, M2_INNER), lambda i, j: (i, 0)),
            pl.BlockSpec((None, D_MODEL, tn), lambda i, j: (layer, 0, j)),
            pl.BlockSpec((tm, tn), lambda i, j: (i, j)),
            pl.BlockSpec((None, 1, tn), lambda i, j: ((i // per_b) * N_MOD + k_gate, 0, j)),
        ],
        out_specs=pl.BlockSpec((tm, tn), lambda i, j: (i, j)),
        out_shape=jax.ShapeDtypeStruct((m, D_MODEL), F32),
        compiler_params=_cparams(2),
        name="out_proj",
    )(y_dn, y_s5, y_m2, w, x, modl)


def _causal_conv(buf_ref, raw, w, t):
    buf_ref[8:t + 8, :] = raw
    acc = raw * w[CONV_K - 1:CONV_K, :]
    for j in range(CONV_K - 1):
        off = 8 - (CONV_K - 1) + j
        acc += buf_ref[off:off + t, :] * w[j:j + 1, :]
    buf_ref[0:8, :] = raw[t - 8:t, :]
    return acc


def _dn_consts():
    idx = np.arange(DN_BLOCK)
    row, col = idx[:, None], idx[None, :]
    blk = lambda s: (row // s) == (col // s)
    strict = row > col
    masks = [blk(DN_BASE) & strict]
    s = DN_BASE
    while s < DN_BLOCK:
        masks.append(blk(2 * s) & ~blk(s) & strict)
        s *= 2
    tri = row >= col
    return (jnp.asarray(tri, BF16), jnp.asarray(tri.T, BF16),
            jnp.asarray(np.stack(masks), F32))


def _dn_kernel(q_ref, k_ref, v_ref, z_ref, sm_ref, smt_ref, wq_ref, wk_ref, wv_ref,
               alog_ref, dtb_ref, nw_ref, tri_ref, trit_ref, mask_ref, o_ref, s_ref, buf_ref):
    tb = DN_BLOCK

    @pl.when(pl.program_id(2) == 0)
    def _():
        s_ref[...] = jnp.zeros_like(s_ref)
        buf_ref[:, 0:8, :] = jnp.zeros((3, 8, DN_HPS * LANE), F32)

    q_all = _silu(_causal_conv(buf_ref.at[0], q_ref[...], wq_ref[...], tb))
    k_all = _silu(_causal_conv(buf_ref.at[1], k_ref[...], wk_ref[...], tb))
    v_all = _silu(_causal_conv(buf_ref.at[2], v_ref[...], wv_ref[...], tb))

    lane = lax.broadcasted_iota(jnp.int32, (1, SMALL_W), 1)
    sm = sm_ref[...]
    pick = lambda idx: jnp.sum(jnp.where(lane == idx, sm, 0.0), axis=1, keepdims=True)
    row = lax.broadcasted_iota(jnp.int32, (tb, tb), 0)
    col = lax.broadcasted_iota(jnp.int32, (tb, tb), 1)
    lower = row >= col
    one = jnp.ones((1, 1), F32)
    n_levels = mask_ref.shape[0]

    def head(i):
        h = pl.program_id(1) * DN_HPS + i
        sl = slice(i * LANE, (i + 1) * LANE)
        q, k, v = q_all[:, sl], k_all[:, sl], v_all[:, sl]
        q = q * lax.rsqrt(jnp.sum(q * q, axis=-1, keepdims=True) + 1e-6) * (DN_DK ** -0.5)
        k = k * lax.rsqrt(jnp.sum(k * k, axis=-1, keepdims=True) + 1e-6)

        coef = -jnp.exp(one * alog_ref[h])
        dtb = one * dtb_ref[h]
        g_c = coef * _softplus(pick(h) + dtb)
        g_r = coef * _softplus(smt_ref[pl.ds(h, 1), :] + dtb)
        beta_c = _sigmoid(pick(DN_HEADS + h))

        gcol = _dot_exact_lhs(tri_ref[...], jnp.broadcast_to(g_c, (tb, LANE)))
        gc_r = _dot_exact_rhs(jnp.broadcast_to(g_r, (8, tb)), trit_ref[...])
        yield
        gc_cc = jnp.concatenate([gcol] * (tb // LANE), axis=1)
        gc_rr = jnp.broadcast_to(gc_r[0:1, :], (tb, tb))
        decay = jnp.exp(jnp.where(lower, gc_cc - gc_rr, -1e30))
        egc = jnp.exp(gcol)

        kb = k * beta_c
        vb = v * beta_c
        kd = _dot_nt(kb, k) * decay
        qk = _dot_nt(q, k) * decay
        yield

        d1 = kd * mask_ref[0]
        d2 = _dot(d1, d1)
        yield
        d4 = _dot(d2, d2)
        yield
        nq = d2 + d4 + _dot(d2, d4)
        yield
        nn = nq - d1 - _dot(d1, nq)
        yield
        for lvl in range(1, n_levels):
            cpart = kd * mask_ref[lvl]
            x = cpart + _dot(cpart, nn)
            yield
            nn = nn - (x + _dot(nn, x))
            yield

        rhs = jnp.concatenate([vb, kb * egc], axis=1)
        sol = rhs + _dot(nn, rhs)
        yield
        u = sol[:, 0:DN_DV]
        w = sol[:, DN_DV:]

        s = s_ref[i]
        sb = s.astype(BF16)
        v_new = u - _dot(w, sb)
        o = _dot(q * egc, sb)
        yield
        o += _dot(qk, v_new)
        g_last = gcol[tb - 1:tb, :]
        s_ref[i] = s * jnp.exp(g_last[:, 0:1]) + _dot_tn(k * jnp.exp(g_last - gcol), v_new)
        yield

        o = o * lax.rsqrt(jnp.mean(o * o, axis=-1, keepdims=True) + NORM_EPS) * nw_ref[...]
        o_ref[:, sl] = (o * _silu(z_ref[:, sl])).astype(o_ref.dtype)

    done = object()
    chains = [head(i) for i in range(DN_HPS)]
    while chains:
        chains = [ch for ch in chains if next(ch, done) is not done]


def _gated_deltanet(proj_dn, small, small_t, conv_w, a_log, dt_bias, norm_w, bsz, seq):
    m = bsz * seq
    tb = DN_BLOCK
    nc = seq // tb
    hg = DN_HEADS // DN_HPS
    wd = DN_HPS * LANE
    tri, tri_t, masks = _dn_consts()
    tok = lambda off: pl.BlockSpec((tb, wd), lambda b, h, c: (b * nc + c, off * hg + h))
    cw = lambda off: pl.BlockSpec((CONV_K, wd), lambda b, h, c: (0, off * hg + h))
    smem = pl.BlockSpec(memory_space=pltpu.SMEM)
    full = lambda *shape: pl.BlockSpec(shape, lambda b, h, c: (0,) * len(shape))
    return pl.pallas_call(
        _dn_kernel,
        grid=(bsz, hg, nc),
        in_specs=[
            tok(0), tok(1), tok(2), tok(3),
            pl.BlockSpec((tb, SMALL_W), lambda b, h, c: (b * nc + c, 0)),
            pl.BlockSpec((None, SMALL_ROWS, tb), lambda b, h, c: (b, 0, c)),
            cw(0), cw(1), cw(2),
            smem, smem,
            full(1, DN_DV), full(tb, tb), full(tb, tb), full(*masks.shape),
        ],
        out_specs=pl.BlockSpec((tb, wd), lambda b, h, c: (b * nc + c, h)),
        out_shape=jax.ShapeDtypeStruct((m, DN_WIDTH), BF16),
        scratch_shapes=[
            pltpu.VMEM((DN_HPS, DN_DK, DN_DV), F32),
            pltpu.VMEM((3, tb + 8, wd), F32),
        ],
        compiler_params=_cparams(3),
        name="gated_deltanet",
    )(proj_dn, proj_dn, proj_dn, proj_dn, small, small_t, conv_w, conv_w, conv_w,
      a_log, dt_bias, norm_w.reshape(1, DN_DV), tri, tri_t, masks)


def _s5_prep_kernel(are_ref, aim_ref, ldt_ref, bre_ref, bim_ref, pr_ref, pi_ref, bbr_ref, bbi_ref):
    lr, li = are_ref[...], aim_ref[...]
    dt = jnp.exp(ldt_ref[...])
    mag = jnp.exp(lr * dt)
    ab_re, ab_im = mag * jnp.cos(li * dt), mag * jnp.sin(li * dt)
    den = lr * lr + li * li
    er, ei = ab_re - 1.0, ab_im
    fr = (lr * er + li * ei) / den
    fi = (lr * ei - li * er) / den
    for c in range(S5_GROUP):
        br, bi = bre_ref[c], bim_ref[c]
        bbr_ref[c] = fr * br - fi * bi
        bbi_ref[c] = fr * bi + fi * br
    pr, pi = ab_re, ab_im
    pr_ref[0], pi_ref[0] = pr, pi
    for n in range(1, S5_ROWS):
        pr, pi = pr * ab_re - pi * ab_im, pr * ab_im + pi * ab_re
        pr_ref[n], pi_ref[n] = pr, pi


def _s5_prep(a_re, a_im, log_dt, b_re, b_im):
    g, p = S5_GROUPS, S5_STATE
    shp = lambda n: jax.ShapeDtypeStruct((n, g, p), F32)
    return pl.pallas_call(
        _s5_prep_kernel,
        out_shape=(shp(S5_ROWS), shp(S5_ROWS), shp(S5_GROUP), shp(S5_GROUP)),
        name="s5_prep",
    )(a_re, a_im, log_dt.reshape(g, 1), jnp.transpose(b_re, (2, 0, 1)), jnp.transpose(b_im, (2, 0, 1)))


def _gelu_tanh(x):
    return 0.5 * x * (1.0 + jnp.tanh(np.sqrt(2.0 / np.pi).astype(np.float32) * (x + 0.044715 * (x * x * x))))


def _s5_kernel(u_ref, wbr_ref, wbi_ref, tr_ref, ti_ref, wcr_ref, wci_ref, d_ref, wglu_ref, bglu_ref,
               o_ref, xr_ref, xi_ref, cr_ref, ci_ref, y_ref):
    tm = S5_BLOCK
    kt = S5_TILE_G * S5_GROUP
    nt = S5_TILE_G * S5_STATE

    @pl.when(pl.program_id(1) == 0)
    def _():
        cr_ref[...] = jnp.zeros_like(cr_ref)
        ci_ref[...] = jnp.zeros_like(ci_ref)

    u = u_ref[...]
    ub = u.astype(BF16)
    for t in range(S5_GROUPS // S5_TILE_G):
        ut = ub[:, t * kt:(t + 1) * kt]
        xr_ref[:, t * nt:(t + 1) * nt] = jnp.dot(ut, wbr_ref[t], preferred_element_type=F32)
        xi_ref[:, t * nt:(t + 1) * nt] = jnp.dot(ut, wbi_ref[t], preferred_element_type=F32)

    def block(i, _):
        r0 = pl.multiple_of(i * S5_ROWS, S5_ROWS)
        for cb in range(S5_LANES // LANE):
            cs = slice(cb * LANE, (cb + 1) * LANE)
            xr = xr_ref[pl.ds(r0, S5_ROWS), cs]
            xi = xi_ref[pl.ds(r0, S5_ROWS), cs]
            for n, d in enumerate((1, 2, 4)):
                ar, ai = tr_ref[n, :, cs], ti_ref[n, :, cs]
                sr, si = pltpu.roll(xr, d, 0), pltpu.roll(xi, d, 0)
                xr, xi = xr + (ar * sr - ai * si), xi + (ar * si + ai * sr)
            pr, pi = tr_ref[3, :, cs], ti_ref[3, :, cs]
            cr, ci = cr_ref[:, cs], ci_ref[:, cs]
            xr, xi = xr + (pr * cr - pi * ci), xi + (pr * ci + pi * cr)
            xr_ref[pl.ds(r0, S5_ROWS), cs] = xr
            xi_ref[pl.ds(r0, S5_ROWS), cs] = xi
            cr_ref[:, cs] = jnp.broadcast_to(xr[S5_ROWS - 1:S5_ROWS, :], (S5_ROWS, LANE))
            ci_ref[:, cs] = jnp.broadcast_to(xi[S5_ROWS - 1:S5_ROWS, :], (S5_ROWS, LANE))
        return 0

    lax.fori_loop(0, tm // S5_ROWS, block, 0)

    for t in range(S5_GROUPS // S5_TILE_G):
        sr = xr_ref[:, t * nt:(t + 1) * nt].astype(BF16)
        si = xi_ref[:, t * nt:(t + 1) * nt].astype(BF16)
        yt = jnp.dot(sr, wcr_ref[t], preferred_element_type=F32)
        yt -= jnp.dot(si, wci_ref[t], preferred_element_type=F32)
        y_ref[:, t * kt:(t + 1) * kt] = yt + d_ref[:, t * kt:(t + 1) * kt] * u[:, t * kt:(t + 1) * kt]

    hg = jnp.dot(_gelu_tanh(y_ref[...]).astype(BF16), wglu_ref[...], preferred_element_type=F32)
    hg += bglu_ref[...]
    o_ref[...] = (hg[:, 0:S5_CH] * _sigmoid(hg[:, S5_CH:])).astype(o_ref.dtype)


def _s5_mixer(u, prep, c_re, c_im, d, w_glu, b_glu, bsz, seq):
    pw_r, pw_i, bb_r, bb_i = prep
    g, p, tg = S5_GROUPS, S5_STATE, S5_TILE_G
    nt = g // tg
    eye = jnp.eye(tg, dtype=F32)
    wb = lambda bb: jnp.einsum("ctip,ij->ticjp", bb.reshape(S5_GROUP, nt, tg, p), eye).reshape(
        nt, tg * S5_GROUP, tg * p).astype(BF16)
    wc = lambda cc: jnp.einsum("ticp,ij->tjpic", cc.reshape(nt, tg, S5_GROUP, p), eye).reshape(
        nt, tg * p, tg * S5_GROUP).astype(BF16)
    pr, pi = pw_r.reshape(S5_ROWS, S5_LANES), pw_i.reshape(S5_ROWS, S5_LANES)
    rows = jnp.arange(S5_ROWS)[:, None]
    tab = lambda pw: jnp.stack(
        [jnp.where(rows >= dd, pw[dd - 1][None, :], 0.0) for dd in (1, 2, 4)] + [pw])
    m = bsz * seq
    tm = S5_BLOCK
    nc = seq // tm
    full = lambda *shape: pl.BlockSpec(shape, lambda b, c: (0,) * len(shape))
    return pl.pallas_call(
        _s5_kernel,
        grid=(bsz, nc),
        in_specs=[
            pl.BlockSpec((tm, S5_CH), lambda b, c: (b * nc + c, 0)),
            full(nt, tg * S5_GROUP, tg * p), full(nt, tg * S5_GROUP, tg * p),
            full(4, S5_ROWS, S5_LANES), full(4, S5_ROWS, S5_LANES),
            full(nt, tg * p, tg * S5_GROUP), full(nt, tg * p, tg * S5_GROUP),
            full(1, S5_CH), full(S5_CH, 2 * S5_CH), full(1, 2 * S5_CH),
        ],
        out_specs=pl.BlockSpec((tm, S5_CH), lambda b, c: (b * nc + c, 0)),
        out_shape=jax.ShapeDtypeStruct((m, S5_CH), BF16),
        scratch_shapes=[
            pltpu.VMEM((tm, S5_LANES), F32), pltpu.VMEM((tm, S5_LANES), F32),
            pltpu.VMEM((S5_ROWS, S5_LANES), F32), pltpu.VMEM((S5_ROWS, S5_LANES), F32),
            pltpu.VMEM((tm, S5_CH), F32),
        ],
        compiler_params=_cparams(2),
        name="s5_mixer",
    )(u, wb(bb_r), wb(bb_i), tab(pr), tab(pi), wc(c_re), wc(c_im),
      d.reshape(1, S5_CH), w_glu.astype(BF16), b_glu.reshape(1, 2 * S5_CH))


def _m2_kernel(p_ref, sm_ref, smt_ref, cw_ref, cb_ref, arow_ref, brow_ref, acol_ref, bcol_ref,
               dexp_ref, nw_ref, o_ref, st_ref, buf_ref):
    t = M2_CHUNK
    n = M2_STATE
    hd = M2_HEADDIM

    @pl.when(pl.program_id(1) == 0)
    def _():
        st_ref[...] = jnp.zeros_like(st_ref)
        buf_ref[0:8, :] = jnp.zeros((8, M2_XBC), F32)

    z = p_ref[:, S5_CH:S5_CH + M2_INNER]
    xbc = _silu(_causal_conv(buf_ref, p_ref[:, S5_CH + M2_INNER:], cw_ref[...], t) + cb_ref[...])
    xs = xbc[:, 0:M2_INNER]
    bm = xbc[:, M2_INNER:M2_INNER + M2_GROUPS * n]
    cm = xbc[:, M2_INNER + M2_GROUPS * n:]

    dt_c = _softplus(sm_ref[...] + brow_ref[...])
    adt_c = dt_c * -jnp.exp(arow_ref[...])
    dt_r = _softplus(smt_ref[2 * DN_HEADS:, :] + bcol_ref[...])
    adt_r = dt_r * -jnp.exp(acol_ref[...])

    row = lax.broadcasted_iota(jnp.int32, (t, t), 0)
    col = lax.broadcasted_iota(jnp.int32, (t, t), 1)
    lower = row >= col
    tri = jnp.where(lower, 1.0, 0.0).astype(BF16)
    tri_t = jnp.where(row <= col, 1.0, 0.0).astype(BF16)
    acs_c = _dot_exact_lhs(tri, adt_c)
    acs_r = _dot_exact_rhs(adt_r, tri_t)

    first_half = lax.broadcasted_iota(jnp.int32, (1, LANE), 1) < hd
    gw = M2_HPG * hd

    def per_head_lanes(cols):
        return jnp.concatenate(
            [jnp.where(first_half, cols[2 * q], cols[2 * q + 1]) for q in range(M2_HPG // 2)], axis=1)

    for g in range(M2_GROUPS):
        sl = slice(g * gw, (g + 1) * gw)
        lanes = [2 * DN_HEADS + g * M2_HPG + j for j in range(M2_HPG)]
        a128 = [jnp.broadcast_to(acs_c[:, ln:ln + 1], (t, LANE)) for ln in lanes]
        acol = per_head_lanes(a128)
        dtx = per_head_lanes([jnp.broadcast_to(dt_c[:, ln:ln + 1], (t, LANE)) for ln in lanes])
        xg = xs[:, sl]
        xdt = xg * dtx
        a_last = acol[t - 1:t, :]

        bg = bm[:, g * n:(g + 1) * n].astype(BF16)
        cg = cm[:, g * n:(g + 1) * n].astype(BF16)
        cbm = lax.dot_general(cg, bg, _NT, preferred_element_type=F32)
        st = st_ref[g]
        y = jnp.dot(cg, st.astype(BF16), preferred_element_type=F32) * jnp.exp(acol)
        y += dexp_ref[:, sl] * xg

        pairs = []
        for q in range(M2_HPG // 2):
            segs = []
            for j in (2 * q, 2 * q + 1):
                hh = g * M2_HPG + j
                seg = jnp.exp(jnp.where(lower, a128[j] - acs_r[hh:hh + 1, :], -1e30))
                segs.append((cbm * seg).astype(BF16))
            xp = xdt[:, q * LANE:(q + 1) * LANE]
            rhs = jnp.concatenate([jnp.where(first_half, xp, 0.0), jnp.where(first_half, 0.0, xp)], axis=0)
            pairs.append(jnp.dot(jnp.concatenate(segs, axis=1), rhs.astype(BF16),
                                 preferred_element_type=F32))
        y += jnp.concatenate(pairs, axis=1)

        xdec = xdt * jnp.exp(a_last - acol)
        st_ref[g] = st * jnp.exp(a_last) + _dot_tn(bg, xdec)

        y = y * _silu(z[:, sl])
        y = y * lax.rsqrt(jnp.mean(y * y, axis=-1, keepdims=True) + NORM_EPS) * nw_ref[:, sl]
        o_ref[:, sl] = y.astype(o_ref.dtype)


def _mamba2(proj_m2, small, small_t, conv_w, conv_b, a_log, dt_bias, d, norm_w, bsz, seq):
    m = bsz * seq
    t = M2_CHUNK
    nc = seq // t
    lo = 2 * DN_HEADS
    lane_pad = lambda v: jnp.zeros((1, SMALL_W), F32).at[0, lo:lo + M2_HEADS].set(v.astype(F32))
    full = lambda *shape: pl.BlockSpec(shape, lambda b, c: (0,) * len(shape))
    return pl.pallas_call(
        _m2_kernel,
        grid=(bsz, nc),
        in_specs=[
            pl.BlockSpec((t, S5_CH + M2_INNER + M2_XBC), lambda b, c: (b * nc + c, 0)),
            pl.BlockSpec((t, SMALL_W), lambda b, c: (b * nc + c, 0)),
            pl.BlockSpec((None, SMALL_ROWS, t), lambda b, c: (b, 0, c)),
            full(CONV_K, M2_XBC), full(1, M2_XBC),
            full(1, SMALL_W), full(1, SMALL_W), full(M2_HEADS, 1), full(M2_HEADS, 1),
            full(1, M2_INNER), full(1, M2_INNER),
        ],
        out_specs=pl.BlockSpec((t, M2_INNER), lambda b, c: (b * nc + c, 0)),
        out_shape=jax.ShapeDtypeStruct((m, M2_INNER), BF16),
        scratch_shapes=[
            pltpu.VMEM((M2_GROUPS, M2_STATE, M2_HPG * M2_HEADDIM), F32),
            pltpu.VMEM((t + 8, M2_XBC), F32),
        ],
        compiler_params=_cparams(2),
        name="mamba2_ssd",
    )(proj_m2, small, small_t, conv_w, conv_b.reshape(1, M2_XBC),
      lane_pad(a_log), lane_pad(dt_bias), a_log.astype(F32).reshape(M2_HEADS, 1),
      dt_bias.astype(F32).reshape(M2_HEADS, 1),
      jnp.repeat(d.astype(F32), M2_HEADDIM).reshape(1, M2_INNER), norm_w.reshape(1, M2_INNER))


def kernel(x, c, w_mod, b_mod, mod_table, norm_ffn1, norm_mix, norm_ffn2, ffn1_w_gate, ffn1_w_up, ffn1_w_down, ffn2_w_gate, ffn2_w_up, ffn2_w_down, w_in, w_out, dn_conv_w, dn_a_log, dn_dt_bias, dn_norm_w, s5_a_re, s5_a_im, s5_b_re, s5_b_im, s5_c_re, s5_c_im, s5_log_dt, s5_d, s5_w_glu, s5_b_glu, m2_conv_w, m2_conv_b, m2_a_log, m2_dt_bias, m2_d, m2_norm_w, final_norm):
    bsz, seq, _ = x.shape
    m = bsz * seq
    mod = _modulation(c, w_mod, b_mod, mod_table)
    xf = x.reshape(m, D_MODEL).astype(F32)

    def ffn(xf, modl, norm_w, wg, wu, wd, layer, k0):
        h = _norm_modulate(xf, norm_w, modl, k0, k0 + 1, seq)
        a, wd_bf16 = _ffn_gateup(h, wg, wu, wd, layer)
        return _ffn_down(a, wd_bf16, xf, modl, k0 + 2, seq)

    o = np.cumsum((0,) + IN_SIZES)
    n_dn = int(o[2])
    n_sm = int(o[7] - o[4])
    w_in_t = jnp.swapaxes(w_in, 1, 2)

    for i in range(DEPTH):
        modl = mod[i]
        xf = ffn(xf, modl, norm_ffn1[i], ffn1_w_gate, ffn1_w_up, ffn1_w_down, i, 0)

        h = _norm_modulate(xf, norm_mix[i], modl, 3, 4, seq)
        proj_dn = _matmul_nt(h, w_in_t, i, 512, 0, n_dn, "in_proj_dn")
        proj_sm = _matmul_nt(h, w_in_t, i, 512, int(o[4]), n_sm, "in_proj_s5_m2")
        small = _in_proj_small(h, w_in_t, i)
        small_t = jnp.transpose(small.reshape(bsz, seq, SMALL_W)[:, :, :SMALL_ROWS], (0, 2, 1))

        y_dn = _gated_deltanet(proj_dn, small, small_t, dn_conv_w[i], dn_a_log[i].astype(F32),
                               dn_dt_bias[i].astype(F32), dn_norm_w[i], bsz, seq)
        prep = _s5_prep(s5_a_re[i], s5_a_im[i], s5_log_dt[i], s5_b_re[i], s5_b_im[i])
        y_s5 = _s5_mixer(proj_sm, prep, s5_c_re[i], s5_c_im[i], s5_d[i], s5_w_glu[i], s5_b_glu[i],
                         bsz, seq)
        y_m2 = _mamba2(proj_sm, small, small_t, m2_conv_w[i], m2_conv_b[i], m2_a_log[i],
                       m2_dt_bias[i], m2_d[i], m2_norm_w[i], bsz, seq)
        xf = _out_proj(y_dn, y_s5, y_m2, w_out, i, xf, modl, 5, seq)

        xf = ffn(xf, modl, norm_ffn2[i], ffn2_w_gate, ffn2_w_up, ffn2_w_down, i, 6)

    return _final_norm(xf, final_norm).reshape(bsz, seq, D_MODEL)
```

```python
import functools

import jax
import jax.numpy as jnp
import numpy as np
from jax import lax
from jax.experimental import pallas as pl
from jax.experimental.pallas import tpu as pltpu

F32 = jnp.float32
BF16 = jnp.bfloat16

D_MODEL = 4096
DEPTH = 2
D_FF = 11008
FFN_RES = 0.5
N_MOD = 9
NORM_EPS = 1e-6
CONV_K = 4

DN_HEADS = 12
DN_DK = 128
DN_DV = 128
DN_BLOCK = 256
DN_BASE = 8
DN_HPS = 6

S5_CH = 1024
S5_GROUP = 16
S5_GROUPS = 64
S5_STATE = 64
S5_LANES = S5_GROUPS * S5_STATE
S5_TILE_G = 8
S5_ROWS = 8
S5_BLOCK = 256

M2_HEADS = 24
M2_HEADDIM = 64
M2_INNER = M2_HEADS * M2_HEADDIM
M2_GROUPS = 4
M2_HPG = M2_HEADS // M2_GROUPS
M2_STATE = 128
M2_CHUNK = 128

DN_WIDTH = DN_HEADS * DN_DV
DN_QKV = 2 * DN_HEADS * DN_DK + DN_HEADS * DN_DV
M2_XBC = M2_INNER + 2 * M2_GROUPS * M2_STATE
IN_SIZES = (DN_QKV, DN_WIDTH, DN_HEADS, DN_HEADS, S5_CH, M2_INNER, M2_XBC, M2_HEADS)
SMALL_W = 128
SMALL_ROWS = 48

LANE = 128
MM_ROWS = 1024
MM_KCHUNK = 1024
VMEM_LIMIT = 56 * 1024 * 1024


def _cparams(n_axes):
    return pltpu.CompilerParams(
        dimension_semantics=("arbitrary",) * n_axes, vmem_limit_bytes=VMEM_LIMIT)


def _sigmoid(x):
    return 1.0 / (1.0 + jnp.exp(-x))


def _silu(x):
    return x * _sigmoid(x)


def _softplus(x):
    return jnp.maximum(x, 0.0) + jnp.log(1.0 + jnp.exp(-jnp.abs(x)))


def _dot(a, b):
    return jnp.dot(a.astype(BF16), b.astype(BF16), preferred_element_type=F32)


def _dot_nt(a, b):
    return lax.dot_general(a.astype(BF16), b.astype(BF16), (((1,), (1,)), ((), ())),
                           preferred_element_type=F32)


def _dot_tn(a, b):
    return lax.dot_general(a.astype(BF16), b.astype(BF16), (((0,), (0,)), ((), ())),
                           preferred_element_type=F32)


def _split3(x):
    x1 = x.astype(BF16)
    r = x - x1.astype(F32)
    x2 = r.astype(BF16)
    x3 = (r - x2.astype(F32)).astype(BF16)
    return x1, x2, x3


def _dot_exact_lhs(mask_bf16, x):
    x1, x2, x3 = _split3(x)
    d = lambda t: jnp.dot(mask_bf16, t, preferred_element_type=F32)
    return d(x1) + d(x2) + d(x3)


def _dot_exact_rhs(x, mask_bf16):
    x1, x2, x3 = _split3(x)
    d = lambda t: jnp.dot(t, mask_bf16, preferred_element_type=F32)
    return d(x1) + d(x2) + d(x3)


def _dot_hi(a, b):
    a1 = a.astype(BF16)
    a2 = (a - a1.astype(F32)).astype(BF16)
    b1 = b.astype(BF16)
    b2 = (b - b1.astype(F32)).astype(BF16)
    d = lambda s, t: jnp.dot(s, t, preferred_element_type=F32)
    return d(a1, b1) + (d(a1, b2) + d(a2, b1))


def _mod_kernel(ct_ref, w_ref, b_ref, t_ref, o_ref, scb_ref):
    bsz = scb_ref.shape[0]

    @pl.when(pl.program_id(0) == 0)
    def _():
        sc = _silu(ct_ref[...])
        for b in range(bsz):
            scb_ref[b] = jnp.broadcast_to(sc[:, b:b + 1], (D_MODEL, LANE))

    for blk in range(w_ref.shape[1] // LANE):
        cs = slice(blk * LANE, (blk + 1) * LANE)
        w = w_ref[:, cs]
        for b in range(bsz):
            acc = jnp.sum(w * scb_ref[b], axis=0, keepdims=True) + b_ref[:, cs]
            for l in range(DEPTH):
                o_ref[l, b:b + 1, cs] = acc + t_ref[l, :, cs]


def _modulation(c, w_mod, b_mod, mod_table):
    bsz = c.shape[0]
    n = N_MOD * D_MODEL
    tn = 512
    out = pl.pallas_call(
        _mod_kernel,
        grid=(n // tn,),
        in_specs=[
            pl.BlockSpec((D_MODEL, bsz), lambda j: (0, 0)),
            pl.BlockSpec((D_MODEL, tn), lambda j: (0, j)),
            pl.BlockSpec((1, tn), lambda j: (0, j)),
            pl.BlockSpec((DEPTH, 1, tn), lambda j: (0, 0, j)),
        ],
        out_specs=pl.BlockSpec((DEPTH, bsz, tn), lambda j: (0, 0, j)),
        out_shape=jax.ShapeDtypeStruct((DEPTH, bsz, n), F32),
        scratch_shapes=[pltpu.VMEM((bsz, D_MODEL, LANE), F32)],
        compiler_params=_cparams(1),
        name="adaln_mod",
    )(c.astype(F32).T, w_mod, b_mod.reshape(1, n), mod_table.reshape(DEPTH, 1, n))
    return out.reshape(DEPTH, bsz * N_MOD, 1, D_MODEL)


def _normmod_kernel(x_ref, w_ref, sh_ref, sc_ref, o_ref):
    x = x_ref[...]
    y = x * lax.rsqrt(jnp.mean(x * x, axis=-1, keepdims=True) + NORM_EPS) * w_ref[...]
    o_ref[...] = (y * (1.0 + sc_ref[...]) + sh_ref[...]).astype(o_ref.dtype)


def _norm_kernel(x_ref, w_ref, o_ref):
    x = x_ref[...]
    y = x * lax.rsqrt(jnp.mean(x * x, axis=-1, keepdims=True) + NORM_EPS) * w_ref[...]
    o_ref[...] = y.astype(o_ref.dtype)


def _norm_modulate(x, w, modl, k_shift, k_scale, seq):
    m = x.shape[0]
    tm = 256
    per_b = seq // tm
    vec = lambda k: pl.BlockSpec((None, 1, D_MODEL), lambda i: ((i // per_b) * N_MOD + k, 0, 0))
    return pl.pallas_call(
        _normmod_kernel,
        grid=(m // tm,),
        in_specs=[
            pl.BlockSpec((tm, D_MODEL), lambda i: (i, 0)),
            pl.BlockSpec((1, D_MODEL), lambda i: (0, 0)),
            vec(k_shift),
            vec(k_scale),
        ],
        out_specs=pl.BlockSpec((tm, D_MODEL), lambda i: (i, 0)),
        out_shape=jax.ShapeDtypeStruct((m, D_MODEL), BF16),
        compiler_params=_cparams(1),
        name="norm_modulate",
    )(x, w.reshape(1, D_MODEL), modl, modl)


def _final_norm(x, w):
    m = x.shape[0]
    tm = 256
    return pl.pallas_call(
        _norm_kernel,
        grid=(m // tm,),
        in_specs=[
            pl.BlockSpec((tm, D_MODEL), lambda i: (i, 0)),
            pl.BlockSpec((1, D_MODEL), lambda i: (0, 0)),
        ],
        out_specs=pl.BlockSpec((tm, D_MODEL), lambda i: (i, 0)),
        out_shape=jax.ShapeDtypeStruct((m, D_MODEL), F32),
        compiler_params=_cparams(1),
        name="final_norm",
    )(x, w.reshape(1, D_MODEL))


_NT = (((1,), (1,)), ((), ()))


def _dot_chunked(a_ref, w_ref, kc):
    acc = None
    for k0 in range(0, a_ref.shape[1], kc):
        p = jnp.dot(a_ref[:, k0:k0 + kc], w_ref[k0:k0 + kc, :].astype(BF16),
                    preferred_element_type=F32)
        acc = p if acc is None else acc + p
    return acc


def _dot_chunked_nt(a_ref, wt_ref, kc):
    acc = None
    for k0 in range(0, a_ref.shape[1], kc):
        p = lax.dot_general(a_ref[:, k0:k0 + kc], wt_ref[:, k0:k0 + kc].astype(BF16), _NT,
                            preferred_element_type=F32)
        acc = p if acc is None else acc + p
    return acc


def _rows_spec(rows, k, index_map):
    return pl.BlockSpec((pl.Element(1), pl.Element(rows), pl.Element(k)), index_map)


def _mm_nt_kernel(a_ref, wt_ref, o_ref):
    o_ref[...] = _dot_chunked_nt(a_ref, wt_ref.at[0], MM_KCHUNK)


def _matmul_nt(a, wt, layer, tn, row0, n, name):
    m, k = a.shape
    tm = min(m, MM_ROWS)
    return pl.pallas_call(
        _mm_nt_kernel,
        grid=(m // tm, n // tn),
        in_specs=[
            pl.BlockSpec((tm, k), lambda i, j: (i, 0)),
            _rows_spec(tn, k, lambda i, j: (layer, pl.multiple_of(row0 + j * tn, 8), 0)),
        ],
        out_specs=pl.BlockSpec((tm, tn), lambda i, j: (i, j)),
        out_shape=jax.ShapeDtypeStruct((m, n), F32),
        compiler_params=_cparams(2),
        name=name,
    )(a, wt)


def _small_kernel(a_ref, wa_ref, wb_ref, o_ref):
    k = a_ref.shape[1]
    pad = jnp.zeros((SMALL_W - SMALL_ROWS, k), F32)
    wt = jnp.concatenate([wa_ref[0], wb_ref[0], pad], axis=0).astype(BF16)
    o_ref[...] = lax.dot_general(a_ref[...], wt, _NT, preferred_element_type=F32)


def _in_proj_small(a, wt, layer):
    m, k = a.shape
    o = np.cumsum((0,) + IN_SIZES)
    half = SMALL_ROWS // 2
    assert int(o[4] - o[2]) == half and int(o[8] - o[7]) == half
    tm = min(m, MM_ROWS)
    grp = lambda r0: _rows_spec(half, k, lambda i: (layer, r0, 0))
    return pl.pallas_call(
        _small_kernel,
        grid=(m // tm,),
        in_specs=[pl.BlockSpec((tm, k), lambda i: (i, 0)), grp(int(o[2])), grp(int(o[7]))],
        out_specs=pl.BlockSpec((tm, SMALL_W), lambda i: (i, 0)),
        out_shape=jax.ShapeDtypeStruct((m, SMALL_W), F32),
        compiler_params=_cparams(1),
        name="in_proj_small",
    )(a, wt, wt)


def _gateup_kernel(h_ref, wg_ref, wu_ref, wd_ref, o_ref, wdb_ref):
    g = _dot_chunked(h_ref, wg_ref, MM_KCHUNK)
    u = _dot_chunked(h_ref, wu_ref, MM_KCHUNK)
    o_ref[...] = (_silu(g) * u).astype(o_ref.dtype)
    wdb_ref[...] = wd_ref[...].astype(BF16)


def _ffn_gateup(h, wg, wu, wd, layer):
    m = h.shape[0]
    tm = min(m, MM_ROWS)
    tn = 256
    ni, nj = m // tm, D_FF // tn
    rows = D_FF // (nj * ni)
    wspec = pl.BlockSpec((None, D_MODEL, tn), lambda i, j: (layer, 0, j))
    return pl.pallas_call(
        _gateup_kernel,
        grid=(ni, nj),
        in_specs=[
            pl.BlockSpec((tm, D_MODEL), lambda i, j: (i, 0)),
            wspec, wspec,
            pl.BlockSpec((None, rows, D_MODEL), lambda i, j: (layer, i * nj + j, 0)),
        ],
        out_specs=(pl.BlockSpec((tm, tn), lambda i, j: (i, j)),
                   pl.BlockSpec((rows, D_MODEL), lambda i, j: (i * nj + j, 0))),
        out_shape=(jax.ShapeDtypeStruct((m, D_FF), BF16),
                   jax.ShapeDtypeStruct((D_FF, D_MODEL), BF16)),
        compiler_params=_cparams(2),
        name="ffn_gateup",
    )(h, wg, wu, wd)


def _down_kernel(a_ref, w_ref, x_ref, g_ref, o_ref):
    acc = jnp.dot(a_ref[...], w_ref[...], preferred_element_type=F32)
    o_ref[...] = x_ref[...] + FFN_RES * g_ref[...] * acc


def _ffn_down(a, wd, x, modl, k_gate, seq):
    m = a.shape[0]
    tm = min(m, MM_ROWS // 2)
    tn = 512
    per_b = seq // tm
    return pl.pallas_call(
        _down_kernel,
        grid=(m // tm, D_MODEL // tn),
        in_specs=[
            pl.BlockSpec((tm, D_FF), lambda i, j: (i, 0)),
            pl.BlockSpec((D_FF, tn), lambda i, j: (0, j)),
            pl.BlockSpec((tm, tn), lambda i, j: (i, j)),
            pl.BlockSpec((None, 1, tn), lambda i, j: ((i // per_b) * N_MOD + k_gate, 0, j)),
        ],
        out_specs=pl.BlockSpec((tm, tn), lambda i, j: (i, j)),
        out_shape=jax.ShapeDtypeStruct((m, D_MODEL), F32),
        compiler_params=_cparams(2),
        name="ffn_down",
    )(a, wd, x, modl)


def _outproj_kernel(ydn_ref, ys5_ref, ym2_ref, w_ref, x_ref, g_ref, o_ref):
    lo = 0
    acc = None
    for y_ref in (ydn_ref, ys5_ref, ym2_ref):
        hi = lo + y_ref.shape[1]
        p = jnp.dot(y_ref[...], w_ref[lo:hi, :].astype(BF16), preferred_element_type=F32)
        acc = p if acc is None else acc + p
        lo = hi
    o_ref[...] = x_ref[...] + g_ref[...] * acc


def _out_proj(y_dn, y_s5, y_m2, w, layer, x, modl, k_gate, seq):
    m = x.shape[0]
    tm = min(m, MM_ROWS)
    tn = 512
    per_b = seq // tm
    return pl.pallas_call(
        _outproj_kernel,
        grid=(m // tm, D_MODEL // tn),
        in_specs=[
            pl.BlockSpec((tm, DN_WIDTH), lambda i, j: (i, 0)),
            pl.BlockSpec((tm, S5_CH), lambda i, j: (i, 0)),
            pl.BlockSpec((tm, M2_INNER), lambda i, j: (i, 0)),
            pl.BlockSpec((None, D_MODEL, tn), lambda i, j: (layer, 0, j)),
            pl.BlockSpec((tm, tn), lambda i, j: (i, j)),
            pl.BlockSpec((None, 1, tn), lambda i, j: ((i // per_b) * N_MOD + k_gate, 0, j)),
        ],
        out_specs=pl.BlockSpec((tm, tn), lambda i, j: (i, j)),
        out_shape=jax.ShapeDtypeStruct((m, D_MODEL), F32),
        compiler_params=_cparams(2),
        name="out_proj",
    )(y_dn, y_s5, y_m2, w, x, modl)


def _causal_conv(buf_ref, raw, w, t):
    buf_ref[8:t + 8, :] = raw
    acc = raw * w[CONV_K - 1:CONV_K, :]
    for j in range(CONV_K - 1):
        off = 8 - (CONV_K - 1) + j
        acc += buf_ref[off:off + t, :] * w[j:j + 1, :]
    buf_ref[0:8, :] = raw[t - 8:t, :]
    return acc


def _dn_consts():
    idx = np.arange(DN_BLOCK)
    row, col = idx[:, None], idx[None, :]
    blk = lambda s: (row // s) == (col // s)
    strict = row > col
    masks = [blk(DN_BASE) & strict]
    s = DN_BASE
    while s < DN_BLOCK:
        masks.append(blk(2 * s) & ~blk(s) & strict)
        s *= 2
    tri = row >= col
    return (jnp.asarray(tri, BF16), jnp.asarray(tri.T, BF16),
            jnp.asarray(np.stack(masks), F32))


def _dn_kernel(q_ref, k_ref, v_ref, z_ref, sm_ref, smt_ref, wq_ref, wk_ref, wv_ref,
               alog_ref, dtb_ref, nw_ref, tri_ref, trit_ref, mask_ref, o_ref, s_ref, buf_ref):
    tb = DN_BLOCK

    @pl.when(pl.program_id(2) == 0)
    def _():
        s_ref[...] = jnp.zeros_like(s_ref)
        buf_ref[:, 0:8, :] = jnp.zeros((3, 8, DN_HPS * LANE), F32)

    q_all = _silu(_causal_conv(buf_ref.at[0], q_ref[...], wq_ref[...], tb))
    k_all = _silu(_causal_conv(buf_ref.at[1], k_ref[...], wk_ref[...], tb))
    v_all = _silu(_causal_conv(buf_ref.at[2], v_ref[...], wv_ref[...], tb))

    lane = lax.broadcasted_iota(jnp.int32, (1, SMALL_W), 1)
    sm = sm_ref[...]
    pick = lambda idx: jnp.sum(jnp.where(lane == idx, sm, 0.0), axis=1, keepdims=True)
    row = lax.broadcasted_iota(jnp.int32, (tb, tb), 0)
    col = lax.broadcasted_iota(jnp.int32, (tb, tb), 1)
    lower = row >= col
    one = jnp.ones((1, 1), F32)
    n_levels = mask_ref.shape[0]

    def head(i):
        h = pl.program_id(1) * DN_HPS + i
        sl = slice(i * LANE, (i + 1) * LANE)
        q, k, v = q_all[:, sl], k_all[:, sl], v_all[:, sl]
        q = q * lax.rsqrt(jnp.sum(q * q, axis=-1, keepdims=True) + 1e-6) * (DN_DK ** -0.5)
        k = k * lax.rsqrt(jnp.sum(k * k, axis=-1, keepdims=True) + 1e-6)

        coef = -jnp.exp(one * alog_ref[h])
        dtb = one * dtb_ref[h]
        g_c = coef * _softplus(pick(h) + dtb)
        g_r = coef * _softplus(smt_ref[pl.ds(h, 1), :] + dtb)
        beta_c = _sigmoid(pick(DN_HEADS + h))

        gcol = _dot_exact_lhs(tri_ref[...], jnp.broadcast_to(g_c, (tb, LANE)))
        gc_r = _dot_exact_rhs(jnp.broadcast_to(g_r, (8, tb)), trit_ref[...])
        yield
        gc_cc = jnp.concatenate([gcol] * (tb // LANE), axis=1)
        gc_rr = jnp.broadcast_to(gc_r[0:1, :], (tb, tb))
        decay = jnp.exp(jnp.where(lower, gc_cc - gc_rr, -1e30))
        egc = jnp.exp(gcol)

        kb = k * beta_c
        vb = v * beta_c
        kd = _dot_nt(kb, k) * decay
        qk = _dot_nt(q, k) * decay
        yield

        d1 = kd * mask_ref[0]
        d2 = _dot(d1, d1)
        yield
        d4 = _dot(d2, d2)
        yield
        nq = d2 + d4 + _dot(d2, d4)
        yield
        nn = nq - d1 - _dot(d1, nq)
        yield
        for lvl in range(1, n_levels):
            cpart = kd * mask_ref[lvl]
            x = cpart + _dot(cpart, nn)
            yield
            nn = nn - (x + _dot(nn, x))
            yield

        rhs = jnp.concatenate([vb, kb * egc], axis=1)
        sol = rhs + _dot(nn, rhs)
        yield
        u = sol[:, 0:DN_DV]
        w = sol[:, DN_DV:]

        s = s_ref[i]
        sb = s.astype(BF16)
        v_new = u - _dot(w, sb)
        o = _dot(q * egc, sb)
        yield
        o += _dot(qk, v_new)
        g_last = gcol[tb - 1:tb, :]
        s_ref[i] = s * jnp.exp(g_last[:, 0:1]) + _dot_tn(k * jnp.exp(g_last - gcol), v_new)
        yield

        o = o * lax.rsqrt(jnp.mean(o * o, axis=-1, keepdims=True) + NORM_EPS) * nw_ref[...]
        o_ref[:, sl] = (o * _silu(z_ref[:, sl])).astype(o_ref.dtype)

    done = object()
    chains = [head(i) for i in range(DN_HPS)]
    while chains:
        chains = [ch for ch in chains if next(ch, done) is not done]


def _gated_deltanet(proj_dn, small, small_t, conv_w, a_log, dt_bias, norm_w, bsz, seq):
    m = bsz * seq
    tb = DN_BLOCK
    nc = seq // tb
    hg = DN_HEADS // DN_HPS
    wd = DN_HPS * LANE
    tri, tri_t, masks = _dn_consts()
    tok = lambda off: pl.BlockSpec((tb, wd), lambda b, h, c: (b * nc + c, off * hg + h))
    cw = lambda off: pl.BlockSpec((CONV_K, wd), lambda b, h, c: (0, off * hg + h))
    smem = pl.BlockSpec(memory_space=pltpu.SMEM)
    full = lambda *shape: pl.BlockSpec(shape, lambda b, h, c: (0,) * len(shape))
    return pl.pallas_call(
        _dn_kernel,
        grid=(bsz, hg, nc),
        in_specs=[
            tok(0), tok(1), tok(2), tok(3),
            pl.BlockSpec((tb, SMALL_W), lambda b, h, c: (b * nc + c, 0)),
            pl.BlockSpec((None, SMALL_ROWS, tb), lambda b, h, c: (b, 0, c)),
            cw(0), cw(1), cw(2),
            smem, smem,
            full(1, DN_DV), full(tb, tb), full(tb, tb), full(*masks.shape),
        ],
        out_specs=pl.BlockSpec((tb, wd), lambda b, h, c: (b * nc + c, h)),
        out_shape=jax.ShapeDtypeStruct((m, DN_WIDTH), BF16),
        scratch_shapes=[
            pltpu.VMEM((DN_HPS, DN_DK, DN_DV), F32),
            pltpu.VMEM((3, tb + 8, wd), F32),
        ],
        compiler_params=_cparams(3),
        name="gated_deltanet",
    )(proj_dn, proj_dn, proj_dn, proj_dn, small, small_t, conv_w, conv_w, conv_w,
      a_log, dt_bias, norm_w.reshape(1, DN_DV), tri, tri_t, masks)


def _s5_prep_kernel(are_ref, aim_ref, ldt_ref, bre_ref, bim_ref, pr_ref, pi_ref, bbr_ref, bbi_ref):
    lr, li = are_ref[...], aim_ref[...]
    dt = jnp.exp(ldt_ref[...])
    mag = jnp.exp(lr * dt)
    ab_re, ab_im = mag * jnp.cos(li * dt), mag * jnp.sin(li * dt)
    den = lr * lr + li * li
    er, ei = ab_re - 1.0, ab_im
    fr = (lr * er + li * ei) / den
    fi = (lr * ei - li * er) / den
    for c in range(S5_GROUP):
        br, bi = bre_ref[c], bim_ref[c]
        bbr_ref[c] = fr * br - fi * bi
        bbi_ref[c] = fr * bi + fi * br
    pr, pi = ab_re, ab_im
    pr_ref[0], pi_ref[0] = pr, pi
    for n in range(1, S5_ROWS):
        pr, pi = pr * ab_re - pi * ab_im, pr * ab_im + pi * ab_re
        pr_ref[n], pi_ref[n] = pr, pi


def _s5_prep(a_re, a_im, log_dt, b_re, b_im):
    g, p = S5_GROUPS, S5_STATE
    shp = lambda n: jax.ShapeDtypeStruct((n, g, p), F32)
    return pl.pallas_call(
        _s5_prep_kernel,
        out_shape=(shp(S5_ROWS), shp(S5_ROWS), shp(S5_GROUP), shp(S5_GROUP)),
        name="s5_prep",
    )(a_re, a_im, log_dt.reshape(g, 1), jnp.transpose(b_re, (2, 0, 1)), jnp.transpose(b_im, (2, 0, 1)))


def _gelu_tanh(x):
    return 0.5 * x * (1.0 + jnp.tanh(np.sqrt(2.0 / np.pi).astype(np.float32) * (x + 0.044715 * (x * x * x))))


def _s5_kernel(u_ref, wbr_ref, wbi_ref, tr_ref, ti_ref, wcr_ref, wci_ref, d_ref, wglu_ref, bglu_ref,
               o_ref, xr_ref, xi_ref, cr_ref, ci_ref, y_ref):
    tm = S5_BLOCK
    kt = S5_TILE_G * S5_GROUP
    nt = S5_TILE_G * S5_STATE

    @pl.when(pl.program_id(1) == 0)
    def _():
        cr_ref[...] = jnp.zeros_like(cr_ref)
        ci_ref[...] = jnp.zeros_like(ci_ref)

    u = u_ref[...]
    ub = u.astype(BF16)
    for t in range(S5_GROUPS // S5_TILE_G):
        ut = ub[:, t * kt:(t + 1) * kt]
        xr_ref[:, t * nt:(t + 1) * nt] = jnp.dot(ut, wbr_ref[t], preferred_element_type=F32)
        xi_ref[:, t * nt:(t + 1) * nt] = jnp.dot(ut, wbi_ref[t], preferred_element_type=F32)

    def block(i, _):
        r0 = pl.multiple_of(i * S5_ROWS, S5_ROWS)
        for cb in range(S5_LANES // LANE):
            cs = slice(cb * LANE, (cb + 1) * LANE)
            xr = xr_ref[pl.ds(r0, S5_ROWS), cs]
            xi = xi_ref[pl.ds(r0, S5_ROWS), cs]
            for n, d in enumerate((1, 2, 4)):
                ar, ai = tr_ref[n, :, cs], ti_ref[n, :, cs]
                sr, si = pltpu.roll(xr, d, 0), pltpu.roll(xi, d, 0)
                xr, xi = xr + (ar * sr - ai * si), xi + (ar * si + ai * sr)
            pr, pi = tr_ref[3, :, cs], ti_ref[3, :, cs]
            cr, ci = cr_ref[:, cs], ci_ref[:, cs]
            xr, xi = xr + (pr * cr - pi * ci), xi + (pr * ci + pi * cr)
            xr_ref[pl.ds(r0, S5_ROWS), cs] = xr
            xi_ref[pl.ds(r0, S5_ROWS), cs] = xi
            cr_ref[:, cs] = jnp.broadcast_to(xr[S5_ROWS - 1:S5_ROWS, :], (S5_ROWS, LANE))
            ci_ref[:, cs] = jnp.broadcast_to(xi[S5_ROWS - 1:S5_ROWS, :], (S5_ROWS, LANE))
        return 0

    lax.fori_loop(0, tm // S5_ROWS, block, 0)

    for t in range(S5_GROUPS // S5_TILE_G):
        sr = xr_ref[:, t * nt:(t + 1) * nt].astype(BF16)
        si = xi_ref[:, t * nt:(t + 1) * nt].astype(BF16)
        yt = jnp.dot(sr, wcr_ref[t], preferred_element_type=F32)
        yt -= jnp.dot(si, wci_ref[t], preferred_element_type=F32)
        y_ref[:, t * kt:(t + 1) * kt] = yt + d_ref[:, t * kt:(t + 1) * kt] * u[:, t * kt:(t + 1) * kt]

    hg = jnp.dot(_gelu_tanh(y_ref[...]).astype(BF16), wglu_ref[...], preferred_element_type=F32)
    hg += bglu_ref[...]
    o_ref[...] = (hg[:, 0:S5_CH] * _sigmoid(hg[:, S5_CH:])).astype(o_ref.dtype)


def _s5_mixer(u, prep, c_re, c_im, d, w_glu, b_glu, bsz, seq):
    pw_r, pw_i, bb_r, bb_i = prep
    g, p, tg = S5_GROUPS, S5_STATE, S5_TILE_G
    nt = g // tg
    eye = jnp.eye(tg, dtype=F32)
    wb = lambda bb: jnp.einsum("ctip,ij->ticjp", bb.reshape(S5_GROUP, nt, tg, p), eye).reshape(
        nt, tg * S5_GROUP, tg * p).astype(BF16)
    wc = lambda cc: jnp.einsum("ticp,ij->tjpic", cc.reshape(nt, tg, S5_GROUP, p), eye).reshape(
        nt, tg * p, tg * S5_GROUP).astype(BF16)
    pr, pi = pw_r.reshape(S5_ROWS, S5_LANES), pw_i.reshape(S5_ROWS, S5_LANES)
    rows = jnp.arange(S5_ROWS)[:, None]
    tab = lambda pw: jnp.stack(
        [jnp.where(rows >= dd, pw[dd - 1][None, :], 0.0) for dd in (1, 2, 4)] + [pw])
    m = bsz * seq
    tm = S5_BLOCK
    nc = seq // tm
    full = lambda *shape: pl.BlockSpec(shape, lambda b, c: (0,) * len(shape))
    return pl.pallas_call(
        _s5_kernel,
        grid=(bsz, nc),
        in_specs=[
            pl.BlockSpec((tm, S5_CH), lambda b, c: (b * nc + c, 0)),
            full(nt, tg * S5_GROUP, tg * p), full(nt, tg * S5_GROUP, tg * p),
            full(4, S5_ROWS, S5_LANES), full(4, S5_ROWS, S5_LANES),
            full(nt, tg * p, tg * S5_GROUP), full(nt, tg * p, tg * S5_GROUP),
            full(1, S5_CH), full(S5_CH, 2 * S5_CH), full(1, 2 * S5_CH),
        ],
        out_specs=pl.BlockSpec((tm, S5_CH), lambda b, c: (b * nc + c, 0)),
        out_shape=jax.ShapeDtypeStruct((m, S5_CH), BF16),
        scratch_shapes=[
            pltpu.VMEM((tm, S5_LANES), F32), pltpu.VMEM((tm, S5_LANES), F32),
            pltpu.VMEM((S5_ROWS, S5_LANES), F32), pltpu.VMEM((S5_ROWS, S5_LANES), F32),
            pltpu.VMEM((tm, S5_CH), F32),
        ],
        compiler_params=_cparams(2),
        name="s5_mixer",
    )(u, wb(bb_r), wb(bb_i), tab(pr), tab(pi), wc(c_re), wc(c_im),
      d.reshape(1, S5_CH), w_glu.astype(BF16), b_glu.reshape(1, 2 * S5_CH))


def _m2_kernel(p_ref, sm_ref, smt_ref, cw_ref, cb_ref, arow_ref, brow_ref, acol_ref, bcol_ref,
               dexp_ref, nw_ref, o_ref, st_ref, buf_ref):
    t = M2_CHUNK
    n = M2_STATE
    hd = M2_HEADDIM

    @pl.when(pl.program_id(1) == 0)
    def _():
        st_ref[...] = jnp.zeros_like(st_ref)
        buf_ref[0:8, :] = jnp.zeros((8, M2_XBC), F32)

    z = p_ref[:, S5_CH:S5_CH + M2_INNER]
    xbc = _silu(_causal_conv(buf_ref, p_ref[:, S5_CH + M2_INNER:], cw_ref[...], t) + cb_ref[...])
    xs = xbc[:, 0:M2_INNER]
    bm = xbc[:, M2_INNER:M2_INNER + M2_GROUPS * n]
    cm = xbc[:, M2_INNER + M2_GROUPS * n:]

    dt_c = _softplus(sm_ref[...] + brow_ref[...])
    adt_c = dt_c * -jnp.exp(arow_ref[...])
    dt_r = _softplus(smt_ref[2 * DN_HEADS:, :] + bcol_ref[...])
    adt_r = dt_r * -jnp.exp(acol_ref[...])

    row = lax.broadcasted_iota(jnp.int32, (t, t), 0)
    col = lax.broadcasted_iota(jnp.int32, (t, t), 1)
    lower = row >= col
    tri = jnp.where(lower, 1.0, 0.0).astype(BF16)
    tri_t = jnp.where(row <= col, 1.0, 0.0).astype(BF16)
    acs_c = _dot_exact_lhs(tri, adt_c)
    acs_r = _dot_exact_rhs(adt_r, tri_t)

    first_half = lax.broadcasted_iota(jnp.int32, (1, LANE), 1) < hd
    gw = M2_HPG * hd

    def per_head_lanes(cols):
        return jnp.concatenate(
            [jnp.where(first_half, cols[2 * q], cols[2 * q + 1]) for q in range(M2_HPG // 2)], axis=1)

    for g in range(M2_GROUPS):
        sl = slice(g * gw, (g + 1) * gw)
        lanes = [2 * DN_HEADS + g * M2_HPG + j for j in range(M2_HPG)]
        a128 = [jnp.broadcast_to(acs_c[:, ln:ln + 1], (t, LANE)) for ln in lanes]
        acol = per_head_lanes(a128)
        dtx = per_head_lanes([jnp.broadcast_to(dt_c[:, ln:ln + 1], (t, LANE)) for ln in lanes])
        xg = xs[:, sl]
        xdt = xg * dtx
        a_last = acol[t - 1:t, :]

        bg = bm[:, g * n:(g + 1) * n].astype(BF16)
        cg = cm[:, g * n:(g + 1) * n].astype(BF16)
        cbm = lax.dot_general(cg, bg, _NT, preferred_element_type=F32)
        st = st_ref[g]
        y = jnp.dot(cg, st.astype(BF16), preferred_element_type=F32) * jnp.exp(acol)
        y += dexp_ref[:, sl] * xg

        pairs = []
        for q in range(M2_HPG // 2):
            segs = []
            for j in (2 * q, 2 * q + 1):
                hh = g * M2_HPG + j
                seg = jnp.exp(jnp.where(lower, a128[j] - acs_r[hh:hh + 1, :], -1e30))
                segs.append((cbm * seg).astype(BF16))
            xp = xdt[:, q * LANE:(q + 1) * LANE]
            rhs = jnp.concatenate([jnp.where(first_half, xp, 0.0), jnp.where(first_half, 0.0, xp)], axis=0)
            pairs.append(jnp.dot(jnp.concatenate(segs, axis=1), rhs.astype(BF16),
                                 preferred_element_type=F32))
        y += jnp.concatenate(pairs, axis=1)

        xdec = xdt * jnp.exp(a_last - acol)
        st_ref[g] = st * jnp.exp(a_last) + _dot_tn(bg, xdec)

        y = y * _silu(z[:, sl])
        y = y * lax.rsqrt(jnp.mean(y * y, axis=-1, keepdims=True) + NORM_EPS) * nw_ref[:, sl]
        o_ref[:, sl] = y.astype(o_ref.dtype)


def _mamba2(proj_m2, small, small_t, conv_w, conv_b, a_log, dt_bias, d, norm_w, bsz, seq):
    m = bsz * seq
    t = M2_CHUNK
    nc = seq // t
    lo = 2 * DN_HEADS
    lane_pad = lambda v: jnp.zeros((1, SMALL_W), F32).at[0, lo:lo + M2_HEADS].set(v.astype(F32))
    full = lambda *shape: pl.BlockSpec(shape, lambda b, c: (0,) * len(shape))
    return pl.pallas_call(
        _m2_kernel,
        grid=(bsz, nc),
        in_specs=[
            pl.BlockSpec((t, S5_CH + M2_INNER + M2_XBC), lambda b, c: (b * nc + c, 0)),
            pl.BlockSpec((t, SMALL_W), lambda b, c: (b * nc + c, 0)),
            pl.BlockSpec((None, SMALL_ROWS, t), lambda b, c: (b, 0, c)),
            full(CONV_K, M2_XBC), full(1, M2_XBC),
            full(1, SMALL_W), full(1, SMALL_W), full(M2_HEADS, 1), full(M2_HEADS, 1),
            full(1, M2_INNER), full(1, M2_INNER),
        ],
        out_specs=pl.BlockSpec((t, M2_INNER), lambda b, c: (b * nc + c, 0)),
        out_shape=jax.ShapeDtypeStruct((m, M2_INNER), BF16),
        scratch_shapes=[
            pltpu.VMEM((M2_GROUPS, M2_STATE, M2_HPG * M2_HEADDIM), F32),
            pltpu.VMEM((t + 8, M2_XBC), F32),
        ],
        compiler_params=_cparams(2),
        name="mamba2_ssd",
    )(proj_m2, small, small_t, conv_w, conv_b.reshape(1, M2_XBC),
      lane_pad(a_log), lane_pad(dt_bias), a_log.astype(F32).reshape(M2_HEADS, 1),
      dt_bias.astype(F32).reshape(M2_HEADS, 1),
      jnp.repeat(d.astype(F32), M2_HEADDIM).reshape(1, M2_INNER), norm_w.reshape(1, M2_INNER))


def kernel(x, c, w_mod, b_mod, mod_table, norm_ffn1, norm_mix, norm_ffn2, ffn1_w_gate, ffn1_w_up, ffn1_w_down, ffn2_w_gate, ffn2_w_up, ffn2_w_down, w_in, w_out, dn_conv_w, dn_a_log, dn_dt_bias, dn_norm_w, s5_a_re, s5_a_im, s5_b_re, s5_b_im, s5_c_re, s5_c_im, s5_log_dt, s5_d, s5_w_glu, s5_b_glu, m2_conv_w, m2_conv_b, m2_a_log, m2_dt_bias, m2_d, m2_norm_w, final_norm):
    bsz, seq, _ = x.shape
    m = bsz * seq
    mod = _modulation(c, w_mod, b_mod, mod_table)
    xf = x.reshape(m, D_MODEL).astype(F32)

    def ffn(xf, modl, norm_w, wg, wu, wd, layer, k0):
        h = _norm_modulate(xf, norm_w, modl, k0, k0 + 1, seq)
        a, wd_bf16 = _ffn_gateup(h, wg, wu, wd, layer)
        return _ffn_down(a, wd_bf16, xf, modl, k0 + 2, seq)

    o = np.cumsum((0,) + IN_SIZES)
    n_dn = int(o[2])
    n_sm = int(o[7] - o[4])
    w_in_t = jnp.swapaxes(w_in, 1, 2)

    for i in range(DEPTH):
        modl = mod[i]
        xf = ffn(xf, modl, norm_ffn1[i], ffn1_w_gate, ffn1_w_up, ffn1_w_down, i, 0)

        h = _norm_modulate(xf, norm_mix[i], modl, 3, 4, seq)
        proj_dn = _matmul_nt(h, w_in_t, i, 512, 0, n_dn, "in_proj_dn")
        proj_sm = _matmul_nt(h, w_in_t, i, 512, int(o[4]), n_sm, "in_proj_s5_m2")
        small = _in_proj_small(h, w_in_t, i)
        small_t = jnp.transpose(small.reshape(bsz, seq, SMALL_W)[:, :, :SMALL_ROWS], (0, 2, 1))

        y_dn = _gated_deltanet(proj_dn, small, small_t, dn_conv_w[i], dn_a_log[i].astype(F32),
                               dn_dt_bias[i].astype(F32), dn_norm_w[i], bsz, seq)
        prep = _s5_prep(s5_a_re[i], s5_a_im[i], s5_log_dt[i], s5_b_re[i], s5_b_im[i])
        y_s5 = _s5_mixer(proj_sm, prep, s5_c_re[i], s5_c_im[i], s5_d[i], s5_w_glu[i], s5_b_glu[i],
                         bsz, seq)
        y_m2 = _mamba2(proj_sm, small, small_t, m2_conv_w[i], m2_conv_b[i], m2_a_log[i],
                       m2_dt_bias[i], m2_d[i], m2_norm_w[i], bsz, seq)
        xf = _out_proj(y_dn, y_s5, y_m2, w_out, i, xf, modl, 5, seq)

        xf = ffn(xf, modl, norm_ffn2[i], ffn2_w_gate, ffn2_w_up, ffn2_w_down, i, 6)

    return _final_norm(xf, final_norm).reshape(bsz, seq, D_MODEL)
```

```python
import functools

import jax
import jax.numpy as jnp
import numpy as np
from jax import lax
from jax.experimental import pallas as pl
from jax.experimental.pallas import tpu as pltpu

F32 = jnp.float32
BF16 = jnp.bfloat16

D_MODEL = 4096
DEPTH = 2
D_FF = 11008
FFN_RES = 0.5
N_MOD = 9
NORM_EPS = 1e-6
CONV_K = 4

DN_HEADS = 12
DN_DK = 128
DN_DV = 128
DN_BLOCK = 256
DN_BASE = 8
DN_HPS = 6

S5_CH = 1024
S5_GROUP = 16
S5_GROUPS = 64
S5_STATE = 64
S5_LANES = S5_GROUPS * S5_STATE
S5_TILE_G = 8
S5_ROWS = 8
S5_BLOCK = 256

M2_HEADS = 24
M2_HEADDIM = 64
M2_INNER = M2_HEADS * M2_HEADDIM
M2_GROUPS = 4
M2_HPG = M2_HEADS // M2_GROUPS
M2_STATE = 128
M2_CHUNK = 128

DN_WIDTH = DN_HEADS * DN_DV
DN_QKV = 2 * DN_HEADS * DN_DK + DN_HEADS * DN_DV
M2_XBC = M2_INNER + 2 * M2_GROUPS * M2_STATE
IN_SIZES = (DN_QKV, DN_WIDTH, DN_HEADS, DN_HEADS, S5_CH, M2_INNER, M2_XBC, M2_HEADS)
SMALL_W = 128
SMALL_ROWS = 48

LANE = 128
MM_ROWS = 1024
MM_KCHUNK = 1024
VMEM_LIMIT = 56 * 1024 * 1024


def _cparams(n_axes):
    return pltpu.CompilerParams(
        dimension_semantics=("arbitrary",) * n_axes, vmem_limit_bytes=VMEM_LIMIT)


def _sigmoid(x):
    return 1.0 / (1.0 + jnp.exp(-x))


def _silu(x):
    return x * _sigmoid(x)


def _softplus(x):
    return jnp.maximum(x, 0.0) + jnp.log(1.0 + jnp.exp(-jnp.abs(x)))


def _dot(a, b):
    return jnp.dot(a.astype(BF16), b.astype(BF16), preferred_element_type=F32)


def _dot_nt(a, b):
    return lax.dot_general(a.astype(BF16), b.astype(BF16), (((1,), (1,)), ((), ())),
                           preferred_element_type=F32)


def _dot_tn(a, b):
    return lax.dot_general(a.astype(BF16), b.astype(BF16), (((0,), (0,)), ((), ())),
                           preferred_element_type=F32)


def _split3(x):
    x1 = x.astype(BF16)
    r = x - x1.astype(F32)
    x2 = r.astype(BF16)
    x3 = (r - x2.astype(F32)).astype(BF16)
    return x1, x2, x3


def _dot_exact_lhs(mask_bf16, x):
    x1, x2, x3 = _split3(x)
    d = lambda t: jnp.dot(mask_bf16, t, preferred_element_type=F32)
    return d(x1) + d(x2) + d(x3)


def _dot_exact_rhs(x, mask_bf16):
    x1, x2, x3 = _split3(x)
    d = lambda t: jnp.dot(t, mask_bf16, preferred_element_type=F32)
    return d(x1) + d(x2) + d(x3)


def _dot_hi(a, b):
    a1 = a.astype(BF16)
    a2 = (a - a1.astype(F32)).astype(BF16)
    b1 = b.astype(BF16)
    b2 = (b - b1.astype(F32)).astype(BF16)
    d = lambda s, t: jnp.dot(s, t, preferred_element_type=F32)
    return d(a1, b1) + (d(a1, b2) + d(a2, b1))


def _mod_kernel(ct_ref, w_ref, b_ref, t_ref, o_ref, scb_ref):
    bsz = scb_ref.shape[0]

    @pl.when(pl.program_id(0) == 0)
    def _():
        sc = _silu(ct_ref[...])
        for b in range(bsz):
            scb_ref[b] = jnp.broadcast_to(sc[:, b:b + 1], (D_MODEL, LANE))

    for blk in range(w_ref.shape[1] // LANE):
        cs = slice(blk * LANE, (blk + 1) * LANE)
        w = w_ref[:, cs]
        for b in range(bsz):
            acc = jnp.sum(w * scb_ref[b], axis=0, keepdims=True) + b_ref[:, cs]
            for l in range(DEPTH):
                o_ref[l, b:b + 1, cs] = acc + t_ref[l, :, cs]


def _modulation(c, w_mod, b_mod, mod_table):
    bsz = c.shape[0]
    n = N_MOD * D_MODEL
    tn = 512
    out = pl.pallas_call(
        _mod_kernel,
        grid=(n // tn,),
        in_specs=[
            pl.BlockSpec((D_MODEL, bsz), lambda j: (0, 0)),
            pl.BlockSpec((D_MODEL, tn), lambda j: (0, j)),
            pl.BlockSpec((1, tn), lambda j: (0, j)),
            pl.BlockSpec((DEPTH, 1, tn), lambda j: (0, 0, j)),
        ],
        out_specs=pl.BlockSpec((DEPTH, bsz, tn), lambda j: (0, 0, j)),
        out_shape=jax.ShapeDtypeStruct((DEPTH, bsz, n), F32),
        scratch_shapes=[pltpu.VMEM((bsz, D_MODEL, LANE), F32)],
        compiler_params=_cparams(1),
        name="adaln_mod",
    )(c.astype(F32).T, w_mod, b_mod.reshape(1, n), mod_table.reshape(DEPTH, 1, n))
    return out.reshape(DEPTH, bsz * N_MOD, 1, D_MODEL)


def _normmod_kernel(x_ref, w_ref, sh_ref, sc_ref, o_ref):
    x = x_ref[...]
    y = x * lax.rsqrt(jnp.mean(x * x, axis=-1, keepdims=True) + NORM_EPS) * w_ref[...]
    o_ref[...] = (y * (1.0 + sc_ref[...]) + sh_ref[...]).astype(o_ref.dtype)


def _norm_kernel(x_ref, w_ref, o_ref):
    x = x_ref[...]
    y = x * lax.rsqrt(jnp.mean(x * x, axis=-1, keepdims=True) + NORM_EPS) * w_ref[...]
    o_ref[...] = y.astype(o_ref.dtype)


def _norm_modulate(x, w, modl, k_shift, k_scale, seq):
    m = x.shape[0]
    tm = 256
    per_b = seq // tm
    vec = lambda k: pl.BlockSpec((None, 1, D_MODEL), lambda i: ((i // per_b) * N_MOD + k, 0, 0))
    return pl.pallas_call(
        _normmod_kernel,
        grid=(m // tm,),
        in_specs=[
            pl.BlockSpec((tm, D_MODEL), lambda i: (i, 0)),
            pl.BlockSpec((1, D_MODEL), lambda i: (0, 0)),
            vec(k_shift),
            vec(k_scale),
        ],
        out_specs=pl.BlockSpec((tm, D_MODEL), lambda i: (i, 0)),
        out_shape=jax.ShapeDtypeStruct((m, D_MODEL), BF16),
        compiler_params=_cparams(1),
        name="norm_modulate",
    )(x, w.reshape(1, D_MODEL), modl, modl)


def _final_norm(x, w):
    m = x.shape[0]
    tm = 256
    return pl.pallas_call(
        _norm_kernel,
        grid=(m // tm,),
        in_specs=[
            pl.BlockSpec((tm, D_MODEL), lambda i: (i, 0)),
            pl.BlockSpec((1, D_MODEL), lambda i: (0, 0)),
        ],
        out_specs=pl.BlockSpec((tm, D_MODEL), lambda i: (i, 0)),
        out_shape=jax.ShapeDtypeStruct((m, D_MODEL), F32),
        compiler_params=_cparams(1),
        name="final_norm",
    )(x, w.reshape(1, D_MODEL))


_NT = (((1,), (1,)), ((), ()))


def _snake(i, j, nj):
    return j + (i % 2) * (nj - 1 - 2 * j)


def _dot_chunked(a_ref, w_ref, kc):
    acc = None
    for k0 in range(0, a_ref.shape[1], kc):
        p = jnp.dot(a_ref[:, k0:k0 + kc], w_ref[k0:k0 + kc, :].astype(BF16),
                    preferred_element_type=F32)
        acc = p if acc is None else acc + p
    return acc


def _dot_chunked_nt(a_ref, wt_ref, kc):
    acc = None
    for k0 in range(0, a_ref.shape[1], kc):
        p = lax.dot_general(a_ref[:, k0:k0 + kc], wt_ref[:, k0:k0 + kc].astype(BF16), _NT,
                            preferred_element_type=F32)
        acc = p if acc is None else acc + p
    return acc


def _rows_spec(rows, k, index_map):
    return pl.BlockSpec((pl.Element(1), pl.Element(rows), pl.Element(k)), index_map)


def _mm_nt_kernel(a_ref, wt_ref, o_ref):
    o_ref[...] = _dot_chunked_nt(a_ref, wt_ref.at[0], MM_KCHUNK)


def _matmul_nt(a, wt, layer, tn, row0, n, name):
    m, k = a.shape
    tm = min(m, MM_ROWS)
    return pl.pallas_call(
        _mm_nt_kernel,
        grid=(m // tm, n // tn),
        in_specs=[
            pl.BlockSpec((tm, k), lambda i, j: (i, 0)),
            _rows_spec(tn, k, lambda i, j: (
                layer, pl.multiple_of(row0 + _snake(i, j, n // tn) * tn, 8), 0)),
        ],
        out_specs=pl.BlockSpec((tm, tn), lambda i, j: (i, _snake(i, j, n // tn))),
        out_shape=jax.ShapeDtypeStruct((m, n), F32),
        compiler_params=_cparams(2),
        name=name,
    )(a, wt)


def _small_kernel(a_ref, wa_ref, wb_ref, o_ref):
    k = a_ref.shape[1]
    pad = jnp.zeros((SMALL_W - SMALL_ROWS, k), F32)
    wt = jnp.concatenate([wa_ref[0], wb_ref[0], pad], axis=0).astype(BF16)
    o_ref[...] = lax.dot_general(a_ref[...], wt, _NT, preferred_element_type=F32)


def _in_proj_small(a, wt, layer):
    m, k = a.shape
    o = np.cumsum((0,) + IN_SIZES)
    half = SMALL_ROWS // 2
    assert int(o[4] - o[2]) == half and int(o[8] - o[7]) == half
    tm = min(m, MM_ROWS)
    grp = lambda r0: _rows_spec(half, k, lambda i: (layer, r0, 0))
    return pl.pallas_call(
        _small_kernel,
        grid=(m // tm,),
        in_specs=[pl.BlockSpec((tm, k), lambda i: (i, 0)), grp(int(o[2])), grp(int(o[7]))],
        out_specs=pl.BlockSpec((tm, SMALL_W), lambda i: (i, 0)),
        out_shape=jax.ShapeDtypeStruct((m, SMALL_W), F32),
        compiler_params=_cparams(1),
        name="in_proj_small",
    )(a, wt, wt)


def _gateup_kernel(h_ref, wg_ref, wu_ref, wd_ref, o_ref, wdb_ref):
    g = _dot_chunked(h_ref, wg_ref, MM_KCHUNK)
    u = _dot_chunked(h_ref, wu_ref, MM_KCHUNK)
    o_ref[...] = (_silu(g) * u).astype(o_ref.dtype)
    wdb_ref[...] = wd_ref[...].astype(BF16)


def _ffn_gateup(h, wg, wu, wd, layer):
    m = h.shape[0]
    tm = min(m, MM_ROWS)
    tn = 256
    ni, nj = m // tm, D_FF // tn
    rows = D_FF // (nj * ni)
    wspec = pl.BlockSpec((None, D_MODEL, tn), lambda i, j: (layer, 0, _snake(i, j, nj)))
    return pl.pallas_call(
        _gateup_kernel,
        grid=(ni, nj),
        in_specs=[
            pl.BlockSpec((tm, D_MODEL), lambda i, j: (i, 0)),
            wspec, wspec,
            pl.BlockSpec((None, rows, D_MODEL), lambda i, j: (layer, i * nj + j, 0)),
        ],
        out_specs=(pl.BlockSpec((tm, tn), lambda i, j: (i, _snake(i, j, nj))),
                   pl.BlockSpec((rows, D_MODEL), lambda i, j: (i * nj + j, 0))),
        out_shape=(jax.ShapeDtypeStruct((m, D_FF), BF16),
                   jax.ShapeDtypeStruct((D_FF, D_MODEL), BF16)),
        compiler_params=_cparams(2),
        name="ffn_gateup",
    )(h, wg, wu, wd)


def _down_kernel(a_ref, w_ref, x_ref, g_ref, o_ref):
    acc = jnp.dot(a_ref[...], w_ref[...], preferred_element_type=F32)
    o_ref[...] = x_ref[...] + FFN_RES * g_ref[...] * acc


def _ffn_down(a, wd, x, modl, k_gate, seq):
    m = a.shape[0]
    tm = min(m, MM_ROWS // 2)
    tn = 512
    nj = D_MODEL // tn
    per_b = seq // tm
    return pl.pallas_call(
        _down_kernel,
        grid=(m // tm, D_MODEL // tn),
        in_specs=[
            pl.BlockSpec((tm, D_FF), lambda i, j: (i, 0)),
            pl.BlockSpec((D_FF, tn), lambda i, j: (0, _snake(i, j, nj))),
            pl.BlockSpec((tm, tn), lambda i, j: (i, _snake(i, j, nj))),
            pl.BlockSpec((None, 1, tn), lambda i, j: (
                (i // per_b) * N_MOD + k_gate, 0, _snake(i, j, nj))),
        ],
        out_specs=pl.BlockSpec((tm, tn), lambda i, j: (i, _snake(i, j, nj))),
        out_shape=jax.ShapeDtypeStruct((m, D_MODEL), F32),
        compiler_params=_cparams(2),
        name="ffn_down",
    )(a, wd, x, modl)


def _outproj_kernel(ydn_ref, ys5_ref, ym2_ref, w_ref, x_ref, g_ref, o_ref):
    lo = 0
    acc = None
    for y_ref in (ydn_ref, ys5_ref, ym2_ref):
        hi = lo + y_ref.shape[1]
        p = jnp.dot(y_ref[...], w_ref[lo:hi, :].astype(BF16), preferred_element_type=F32)
        acc = p if acc is None else acc + p
        lo = hi
    o_ref[...] = x_ref[...] + g_ref[...] * acc


def _out_proj(y_dn, y_s5, y_m2, w, layer, x, modl, k_gate, seq):
    m = x.shape[0]
    tm = min(m, MM_ROWS)
    tn = 512
    nj = D_MODEL // tn
    per_b = seq // tm
    return pl.pallas_call(
        _outproj_kernel,
        grid=(m // tm, D_MODEL // tn),
        in_specs=[
            pl.BlockSpec((tm, DN_WIDTH), lambda i, j: (i, 0)),
            pl.BlockSpec((tm, S5_CH), lambda i, j: (i, 0)),
            pl.BlockSpec((tm, M2_INNER), lambda i, j: (i, 0)),
            pl.BlockSpec((None, D_MODEL, tn), lambda i, j: (layer, 0, _snake(i, j, nj))),
            pl.BlockSpec((tm, tn), lambda i, j: (i, _snake(i, j, nj))),
            pl.BlockSpec((None, 1, tn), lambda i, j: (
                (i // per_b) * N_MOD + k_gate, 0, _snake(i, j, nj))),
        ],
        out_specs=pl.BlockSpec((tm, tn), lambda i, j: (i, _snake(i, j, nj))),
        out_shape=jax.ShapeDtypeStruct((m, D_MODEL), F32),
        compiler_params=_cparams(2),
        name="out_proj",
    )(y_dn, y_s5, y_m2, w, x, modl)


def _causal_conv(buf_ref, raw, w, t):
    buf_ref[8:t + 8, :] = raw
    acc = raw * w[CONV_K - 1:CONV_K, :]
    for j in range(CONV_K - 1):
        off = 8 - (CONV_K - 1) + j
        acc += buf_ref[off:off + t, :] * w[j:j + 1, :]
    buf_ref[0:8, :] = raw[t - 8:t, :]
    return acc


def _dn_consts():
    idx = np.arange(DN_BLOCK)
    row, col = idx[:, None], idx[None, :]
    blk = lambda s: (row // s) == (col // s)
    strict = row > col
    masks = [blk(DN_BASE) & strict]
    s = DN_BASE
    while s < DN_BLOCK:
        masks.append(blk(2 * s) & ~blk(s) & strict)
        s *= 2
    tri = row >= col
    return (jnp.asarray(tri, BF16), jnp.asarray(tri.T, BF16),
            jnp.asarray(np.stack(masks), F32))


def _dn_kernel(q_ref, k_ref, v_ref, z_ref, sm_ref, smt_ref, wq_ref, wk_ref, wv_ref,
               alog_ref, dtb_ref, nw_ref, tri_ref, trit_ref, mask_ref, o_ref, s_ref, buf_ref):
    tb = DN_BLOCK

    @pl.when(pl.program_id(2) == 0)
    def _():
        s_ref[...] = jnp.zeros_like(s_ref)
        buf_ref[:, 0:8, :] = jnp.zeros((3, 8, DN_HPS * LANE), F32)

    q_all = _silu(_causal_conv(buf_ref.at[0], q_ref[...], wq_ref[...], tb))
    k_all = _silu(_causal_conv(buf_ref.at[1], k_ref[...], wk_ref[...], tb))
    v_all = _silu(_causal_conv(buf_ref.at[2], v_ref[...], wv_ref[...], tb))

    lane = lax.broadcasted_iota(jnp.int32, (1, SMALL_W), 1)
    sm = sm_ref[...]
    pick = lambda idx: jnp.sum(jnp.where(lane == idx, sm, 0.0), axis=1, keepdims=True)
    row = lax.broadcasted_iota(jnp.int32, (tb, tb), 0)
    col = lax.broadcasted_iota(jnp.int32, (tb, tb), 1)
    lower = row >= col
    one = jnp.ones((1, 1), F32)
    n_levels = mask_ref.shape[0]

    def head(i):
        h = pl.program_id(1) * DN_HPS + i
        sl = slice(i * LANE, (i + 1) * LANE)
        q, k, v = q_all[:, sl], k_all[:, sl], v_all[:, sl]
        q = q * lax.rsqrt(jnp.sum(q * q, axis=-1, keepdims=True) + 1e-6) * (DN_DK ** -0.5)
        k = k * lax.rsqrt(jnp.sum(k * k, axis=-1, keepdims=True) + 1e-6)

        coef = -jnp.exp(one * alog_ref[h])
        dtb = one * dtb_ref[h]
        g_c = coef * _softplus(pick(h) + dtb)
        g_r = coef * _softplus(smt_ref[pl.ds(h, 1), :] + dtb)
        beta_c = _sigmoid(pick(DN_HEADS + h))

        gcol = _dot_exact_lhs(tri_ref[...], jnp.broadcast_to(g_c, (tb, LANE)))
        gc_r = _dot_exact_rhs(jnp.broadcast_to(g_r, (8, tb)), trit_ref[...])
        yield
        gc_cc = jnp.concatenate([gcol] * (tb // LANE), axis=1)
        gc_rr = jnp.broadcast_to(gc_r[0:1, :], (tb, tb))
        decay = jnp.exp(jnp.where(lower, gc_cc - gc_rr, -1e30))
        egc = jnp.exp(gcol)

        kb = k * beta_c
        vb = v * beta_c
        kd = _dot_nt(kb, k) * decay
        qk = _dot_nt(q, k) * decay
        yield

        d1 = kd * mask_ref[0]
        d2 = _dot(d1, d1)
        yield
        d4 = _dot(d2, d2)
        yield
        nq = d2 + d4 + _dot(d2, d4)
        yield
        nn = nq - d1 - _dot(d1, nq)
        yield
        for lvl in range(1, n_levels):
            cpart = kd * mask_ref[lvl]
            x = cpart + _dot(cpart, nn)
            yield
            nn = nn - (x + _dot(nn, x))
            yield

        rhs = jnp.concatenate([vb, kb * egc], axis=1)
        sol = rhs + _dot(nn, rhs)
        yield
        u = sol[:, 0:DN_DV]
        w = sol[:, DN_DV:]

        s = s_ref[i]
        sb = s.astype(BF16)
        v_new = u - _dot(w, sb)
        o = _dot(q * egc, sb)
        yield
        o += _dot(qk, v_new)
        g_last = gcol[tb - 1:tb, :]
        s_ref[i] = s * jnp.exp(g_last[:, 0:1]) + _dot_tn(k * jnp.exp(g_last - gcol), v_new)
        yield

        o = o * lax.rsqrt(jnp.mean(o * o, axis=-1, keepdims=True) + NORM_EPS) * nw_ref[...]
        o_ref[:, sl] = (o * _silu(z_ref[:, sl])).astype(o_ref.dtype)

    done = object()
    chains = [head(i) for i in range(DN_HPS)]
    while chains:
        chains = [ch for ch in chains if next(ch, done) is not done]


def _gated_deltanet(proj_dn, small, small_t, conv_w, a_log, dt_bias, norm_w, bsz, seq):
    m = bsz * seq
    tb = DN_BLOCK
    nc = seq // tb
    hg = DN_HEADS // DN_HPS
    wd = DN_HPS * LANE
    tri, tri_t, masks = _dn_consts()
    tok = lambda off: pl.BlockSpec((tb, wd), lambda b, h, c: (b * nc + c, off * hg + h))
    cw = lambda off: pl.BlockSpec((CONV_K, wd), lambda b, h, c: (0, off * hg + h))
    smem = pl.BlockSpec(memory_space=pltpu.SMEM)
    full = lambda *shape: pl.BlockSpec(shape, lambda b, h, c: (0,) * len(shape))
    return pl.pallas_call(
        _dn_kernel,
        grid=(bsz, hg, nc),
        in_specs=[
            tok(0), tok(1), tok(2), tok(3),
            pl.BlockSpec((tb, SMALL_W), lambda b, h, c: (b * nc + c, 0)),
            pl.BlockSpec((None, SMALL_ROWS, tb), lambda b, h, c: (b, 0, c)),
            cw(0), cw(1), cw(2),
            smem, smem,
            full(1, DN_DV), full(tb, tb), full(tb, tb), full(*masks.shape),
        ],
        out_specs=pl.BlockSpec((tb, wd), lambda b, h, c: (b * nc + c, h)),
        out_shape=jax.ShapeDtypeStruct((m, DN_WIDTH), BF16),
        scratch_shapes=[
            pltpu.VMEM((DN_HPS, DN_DK, DN_DV), F32),
            pltpu.VMEM((3, tb + 8, wd), F32),
        ],
        compiler_params=_cparams(3),
        name="gated_deltanet",
    )(proj_dn, proj_dn, proj_dn, proj_dn, small, small_t, conv_w, conv_w, conv_w,
      a_log, dt_bias, norm_w.reshape(1, DN_DV), tri, tri_t, masks)


def _s5_prep_kernel(are_ref, aim_ref, ldt_ref, bre_ref, bim_ref, pr_ref, pi_ref, bbr_ref, bbi_ref):
    lr, li = are_ref[...], aim_ref[...]
    dt = jnp.exp(ldt_ref[...])
    mag = jnp.exp(lr * dt)
    ab_re, ab_im = mag * jnp.cos(li * dt), mag * jnp.sin(li * dt)
    den = lr * lr + li * li
    er, ei = ab_re - 1.0, ab_im
    fr = (lr * er + li * ei) / den
    fi = (lr * ei - li * er) / den
    for c in range(S5_GROUP):
        br, bi = bre_ref[c], bim_ref[c]
        bbr_ref[c] = fr * br - fi * bi
        bbi_ref[c] = fr * bi + fi * br
    pr, pi = ab_re, ab_im
    pr_ref[0], pi_ref[0] = pr, pi
    for n in range(1, S5_ROWS):
        pr, pi = pr * ab_re - pi * ab_im, pr * ab_im + pi * ab_re
        pr_ref[n], pi_ref[n] = pr, pi


def _s5_prep(a_re, a_im, log_dt, b_re, b_im):
    g, p = S5_GROUPS, S5_STATE
    shp = lambda n: jax.ShapeDtypeStruct((n, g, p), F32)
    return pl.pallas_call(
        _s5_prep_kernel,
        out_shape=(shp(S5_ROWS), shp(S5_ROWS), shp(S5_GROUP), shp(S5_GROUP)),
        name="s5_prep",
    )(a_re, a_im, log_dt.reshape(g, 1), jnp.transpose(b_re, (2, 0, 1)), jnp.transpose(b_im, (2, 0, 1)))


def _gelu_tanh(x):
    return 0.5 * x * (1.0 + jnp.tanh(np.sqrt(2.0 / np.pi).astype(np.float32) * (x + 0.044715 * (x * x * x))))


def _s5_kernel(u_ref, wbr_ref, wbi_ref, tr_ref, ti_ref, wcr_ref, wci_ref, d_ref, wglu_ref, bglu_ref,
               o_ref, xr_ref, xi_ref, cr_ref, ci_ref, y_ref):
    tm = S5_BLOCK
    kt = S5_TILE_G * S5_GROUP
    nt = S5_TILE_G * S5_STATE

    @pl.when(pl.program_id(1) == 0)
    def _():
        cr_ref[...] = jnp.zeros_like(cr_ref)
        ci_ref[...] = jnp.zeros_like(ci_ref)

    u = u_ref[...]
    ub = u.astype(BF16)
    for t in range(S5_GROUPS // S5_TILE_G):
        ut = ub[:, t * kt:(t + 1) * kt]
        xr_ref[:, t * nt:(t + 1) * nt] = jnp.dot(ut, wbr_ref[t], preferred_element_type=F32)
        xi_ref[:, t * nt:(t + 1) * nt] = jnp.dot(ut, wbi_ref[t], preferred_element_type=F32)

    def block(i, _):
        r0 = pl.multiple_of(i * S5_ROWS, S5_ROWS)
        for cb in range(S5_LANES // LANE):
            cs = slice(cb * LANE, (cb + 1) * LANE)
            xr = xr_ref[pl.ds(r0, S5_ROWS), cs]
            xi = xi_ref[pl.ds(r0, S5_ROWS), cs]
            for n, d in enumerate((1, 2, 4)):
                ar, ai = tr_ref[n, :, cs], ti_ref[n, :, cs]
                sr, si = pltpu.roll(xr, d, 0), pltpu.roll(xi, d, 0)
                xr, xi = xr + (ar * sr - ai * si), xi + (ar * si + ai * sr)
            pr, pi = tr_ref[3, :, cs], ti_ref[3, :, cs]
            cr, ci = cr_ref[:, cs], ci_ref[:, cs]
            xr, xi = xr + (pr * cr - pi * ci), xi + (pr * ci + pi * cr)
            xr_ref[pl.ds(r0, S5_ROWS), cs] = xr
            xi_ref[pl.ds(r0, S5_ROWS), cs] = xi
            cr_ref[:, cs] = jnp.broadcast_to(xr[S5_ROWS - 1:S5_ROWS, :], (S5_ROWS, LANE))
            ci_ref[:, cs] = jnp.broadcast_to(xi[S5_ROWS - 1:S5_ROWS, :], (S5_ROWS, LANE))
        return 0

    lax.fori_loop(0, tm // S5_ROWS, block, 0)

    for t in range(S5_GROUPS // S5_TILE_G):
        sr = xr_ref[:, t * nt:(t + 1) * nt].astype(BF16)
        si = xi_ref[:, t * nt:(t + 1) * nt].astype(BF16)
        yt = jnp.dot(sr, wcr_ref[t], preferred_element_type=F32)
        yt -= jnp.dot(si, wci_ref[t], preferred_element_type=F32)
        y_ref[:, t * kt:(t + 1) * kt] = yt + d_ref[:, t * kt:(t + 1) * kt] * u[:, t * kt:(t + 1) * kt]

    hg = jnp.dot(_gelu_tanh(y_ref[...]).astype(BF16), wglu_ref[...], preferred_element_type=F32)
    hg += bglu_ref[...]
    o_ref[...] = (hg[:, 0:S5_CH] * _sigmoid(hg[:, S5_CH:])).astype(o_ref.dtype)


def _s5_mixer(u, prep, c_re, c_im, d, w_glu, b_glu, bsz, seq):
    pw_r, pw_i, bb_r, bb_i = prep
    g, p, tg = S5_GROUPS, S5_STATE, S5_TILE_G
    nt = g // tg
    eye = jnp.eye(tg, dtype=F32)
    wb = lambda bb: jnp.einsum("ctip,ij->ticjp", bb.reshape(S5_GROUP, nt, tg, p), eye).reshape(
        nt, tg * S5_GROUP, tg * p).astype(BF16)
    wc = lambda cc: jnp.einsum("ticp,ij->tjpic", cc.reshape(nt, tg, S5_GROUP, p), eye).reshape(
        nt, tg * p, tg * S5_GROUP).astype(BF16)
    pr, pi = pw_r.reshape(S5_ROWS, S5_LANES), pw_i.reshape(S5_ROWS, S5_LANES)
    rows = jnp.arange(S5_ROWS)[:, None]
    tab = lambda pw: jnp.stack(
        [jnp.where(rows >= dd, pw[dd - 1][None, :], 0.0) for dd in (1, 2, 4)] + [pw])
    m = bsz * seq
    tm = S5_BLOCK
    nc = seq // tm
    full = lambda *shape: pl.BlockSpec(shape, lambda b, c: (0,) * len(shape))
    return pl.pallas_call(
        _s5_kernel,
        grid=(bsz, nc),
        in_specs=[
            pl.BlockSpec((tm, S5_CH), lambda b, c: (b * nc + c, 0)),
            full(nt, tg * S5_GROUP, tg * p), full(nt, tg * S5_GROUP, tg * p),
            full(4, S5_ROWS, S5_LANES), full(4, S5_ROWS, S5_LANES),
            full(nt, tg * p, tg * S5_GROUP), full(nt, tg * p, tg * S5_GROUP),
            full(1, S5_CH), full(S5_CH, 2 * S5_CH), full(1, 2 * S5_CH),
        ],
        out_specs=pl.BlockSpec((tm, S5_CH), lambda b, c: (b * nc + c, 0)),
        out_shape=jax.ShapeDtypeStruct((m, S5_CH), BF16),
        scratch_shapes=[
            pltpu.VMEM((tm, S5_LANES), F32), pltpu.VMEM((tm, S5_LANES), F32),
            pltpu.VMEM((S5_ROWS, S5_LANES), F32), pltpu.VMEM((S5_ROWS, S5_LANES), F32),
            pltpu.VMEM((tm, S5_CH), F32),
        ],
        compiler_params=_cparams(2),
        name="s5_mixer",
    )(u, wb(bb_r), wb(bb_i), tab(pr), tab(pi), wc(c_re), wc(c_im),
      d.reshape(1, S5_CH), w_glu.astype(BF16), b_glu.reshape(1, 2 * S5_CH))


def _m2_kernel(p_ref, sm_ref, smt_ref, cw_ref, cb_ref, arow_ref, brow_ref, acol_ref, bcol_ref,
               dexp_ref, nw_ref, o_ref, st_ref, buf_ref):
    t = M2_CHUNK
    n = M2_STATE
    hd = M2_HEADDIM

    @pl.when(pl.program_id(1) == 0)
    def _():
        st_ref[...] = jnp.zeros_like(st_ref)
        buf_ref[0:8, :] = jnp.zeros((8, M2_XBC), F32)

    z = p_ref[:, S5_CH:S5_CH + M2_INNER]
    xbc = _silu(_causal_conv(buf_ref, p_ref[:, S5_CH + M2_INNER:], cw_ref[...], t) + cb_ref[...])
    xs = xbc[:, 0:M2_INNER]
    bm = xbc[:, M2_INNER:M2_INNER + M2_GROUPS * n]
    cm = xbc[:, M2_INNER + M2_GROUPS * n:]

    dt_c = _softplus(sm_ref[...] + brow_ref[...])
    adt_c = dt_c * -jnp.exp(arow_ref[...])
    dt_r = _softplus(smt_ref[2 * DN_HEADS:, :] + bcol_ref[...])
    adt_r = dt_r * -jnp.exp(acol_ref[...])

    row = lax.broadcasted_iota(jnp.int32, (t, t), 0)
    col = lax.broadcasted_iota(jnp.int32, (t, t), 1)
    lower = row >= col
    tri = jnp.where(lower, 1.0, 0.0).astype(BF16)
    tri_t = jnp.where(row <= col, 1.0, 0.0).astype(BF16)
    acs_c = _dot_exact_lhs(tri, adt_c)
    acs_r = _dot_exact_rhs(adt_r, tri_t)

    first_half = lax.broadcasted_iota(jnp.int32, (1, LANE), 1) < hd
    gw = M2_HPG * hd

    def per_head_lanes(cols):
        return jnp.concatenate(
            [jnp.where(first_half, cols[2 * q], cols[2 * q + 1]) for q in range(M2_HPG // 2)], axis=1)

    for g in range(M2_GROUPS):
        sl = slice(g * gw, (g + 1) * gw)
        lanes = [2 * DN_HEADS + g * M2_HPG + j for j in range(M2_HPG)]
        a128 = [jnp.broadcast_to(acs_c[:, ln:ln + 1], (t, LANE)) for ln in lanes]
        acol = per_head_lanes(a128)
        dtx = per_head_lanes([jnp.broadcast_to(dt_c[:, ln:ln + 1], (t, LANE)) for ln in lanes])
        xg = xs[:, sl]
        xdt = xg * dtx
        a_last = acol[t - 1:t, :]

        bg = bm[:, g * n:(g + 1) * n].astype(BF16)
        cg = cm[:, g * n:(g + 1) * n].astype(BF16)
        cbm = lax.dot_general(cg, bg, _NT, preferred_element_type=F32)
        st = st_ref[g]
        y = jnp.dot(cg, st.astype(BF16), preferred_element_type=F32) * jnp.exp(acol)
        y += dexp_ref[:, sl] * xg

        pairs = []
        for q in range(M2_HPG // 2):
            segs = []
            for j in (2 * q, 2 * q + 1):
                hh = g * M2_HPG + j
                seg = jnp.exp(jnp.where(lower, a128[j] - acs_r[hh:hh + 1, :], -1e30))
                segs.append((cbm * seg).astype(BF16))
            xp = xdt[:, q * LANE:(q + 1) * LANE]
            rhs = jnp.concatenate([jnp.where(first_half, xp, 0.0), jnp.where(first_half, 0.0, xp)], axis=0)
            pairs.append(jnp.dot(jnp.concatenate(segs, axis=1), rhs.astype(BF16),
                                 preferred_element_type=F32))
        y += jnp.concatenate(pairs, axis=1)

        xdec = xdt * jnp.exp(a_last - acol)
        st_ref[g] = st * jnp.exp(a_last) + _dot_tn(bg, xdec)

        y = y * _silu(z[:, sl])
        y = y * lax.rsqrt(jnp.mean(y * y, axis=-1, keepdims=True) + NORM_EPS) * nw_ref[:, sl]
        o_ref[:, sl] = y.astype(o_ref.dtype)


def _mamba2(proj_m2, small, small_t, conv_w, conv_b, a_log, dt_bias, d, norm_w, bsz, seq):
    m = bsz * seq
    t = M2_CHUNK
    nc = seq // t
    lo = 2 * DN_HEADS
    lane_pad = lambda v: jnp.zeros((1, SMALL_W), F32).at[0, lo:lo + M2_HEADS].set(v.astype(F32))
    full = lambda *shape: pl.BlockSpec(shape, lambda b, c: (0,) * len(shape))
    return pl.pallas_call(
        _m2_kernel,
        grid=(bsz, nc),
        in_specs=[
            pl.BlockSpec((t, S5_CH + M2_INNER + M2_XBC), lambda b, c: (b * nc + c, 0)),
            pl.BlockSpec((t, SMALL_W), lambda b, c: (b * nc + c, 0)),
            pl.BlockSpec((None, SMALL_ROWS, t), lambda b, c: (b, 0, c)),
            full(CONV_K, M2_XBC), full(1, M2_XBC),
            full(1, SMALL_W), full(1, SMALL_W), full(M2_HEADS, 1), full(M2_HEADS, 1),
            full(1, M2_INNER), full(1, M2_INNER),
        ],
        out_specs=pl.BlockSpec((t, M2_INNER), lambda b, c: (b * nc + c, 0)),
        out_shape=jax.ShapeDtypeStruct((m, M2_INNER), BF16),
        scratch_shapes=[
            pltpu.VMEM((M2_GROUPS, M2_STATE, M2_HPG * M2_HEADDIM), F32),
            pltpu.VMEM((t + 8, M2_XBC), F32),
        ],
        compiler_params=_cparams(2),
        name="mamba2_ssd",
    )(proj_m2, small, small_t, conv_w, conv_b.reshape(1, M2_XBC),
      lane_pad(a_log), lane_pad(dt_bias), a_log.astype(F32).reshape(M2_HEADS, 1),
      dt_bias.astype(F32).reshape(M2_HEADS, 1),
      jnp.repeat(d.astype(F32), M2_HEADDIM).reshape(1, M2_INNER), norm_w.reshape(1, M2_INNER))


def kernel(x, c, w_mod, b_mod, mod_table, norm_ffn1, norm_mix, norm_ffn2, ffn1_w_gate, ffn1_w_up, ffn1_w_down, ffn2_w_gate, ffn2_w_up, ffn2_w_down, w_in, w_out, dn_conv_w, dn_a_log, dn_dt_bias, dn_norm_w, s5_a_re, s5_a_im, s5_b_re, s5_b_im, s5_c_re, s5_c_im, s5_log_dt, s5_d, s5_w_glu, s5_b_glu, m2_conv_w, m2_conv_b, m2_a_log, m2_dt_bias, m2_d, m2_norm_w, final_norm):
    bsz, seq, _ = x.shape
    m = bsz * seq
    mod = _modulation(c, w_mod, b_mod, mod_table)
    xf = x.reshape(m, D_MODEL).astype(F32)

    def ffn(xf, modl, norm_w, wg, wu, wd, layer, k0):
        h = _norm_modulate(xf, norm_w, modl, k0, k0 + 1, seq)
        a, wd_bf16 = _ffn_gateup(h, wg, wu, wd, layer)
        return _ffn_down(a, wd_bf16, xf, modl, k0 + 2, seq)

    o = np.cumsum((0,) + IN_SIZES)
    n_dn = int(o[2])
    n_sm = int(o[7] - o[4])
    w_in_t = jnp.swapaxes(w_in, 1, 2)

    for i in range(DEPTH):
        modl = mod[i]
        xf = ffn(xf, modl, norm_ffn1[i], ffn1_w_gate, ffn1_w_up, ffn1_w_down, i, 0)

        h = _norm_modulate(xf, norm_mix[i], modl, 3, 4, seq)
        proj_dn = _matmul_nt(h, w_in_t, i, 512, 0, n_dn, "in_proj_dn")
        proj_sm = _matmul_nt(h, w_in_t, i, 512, int(o[4]), n_sm, "in_proj_s5_m2")
        small = _in_proj_small(h, w_in_t, i)
        small_t = jnp.transpose(small.reshape(bsz, seq, SMALL_W)[:, :, :SMALL_ROWS], (0, 2, 1))

        y_dn = _gated_deltanet(proj_dn, small, small_t, dn_conv_w[i], dn_a_log[i].astype(F32),
                               dn_dt_bias[i].astype(F32), dn_norm_w[i], bsz, seq)
        prep = _s5_prep(s5_a_re[i], s5_a_im[i], s5_log_dt[i], s5_b_re[i], s5_b_im[i])
        y_s5 = _s5_mixer(proj_sm, prep, s5_c_re[i], s5_c_im[i], s5_d[i], s5_w_glu[i], s5_b_glu[i],
                         bsz, seq)
        y_m2 = _mamba2(proj_sm, small, small_t, m2_conv_w[i], m2_conv_b[i], m2_a_log[i],
                       m2_dt_bias[i], m2_d[i], m2_norm_w[i], bsz, seq)
        xf = _out_proj(y_dn, y_s5, y_m2, w_out, i, xf, modl, 5, seq)

        xf = ffn(xf, modl, norm_ffn2[i], ffn2_w_gate, ffn2_w_up, ffn2_w_down, i, 6)

    return _final_norm(xf, final_norm).reshape(bsz, seq, D_MODEL)
```
